```python
import math
import jax
import jax.numpy as jnp
from jax import lax
import numpy as np


D_MODEL = 1024
BATCH = 16
SEQ = 4096
DEPTH = 4

MLA_HEADS = 8
MLA_Q_LORA = 256
MLA_KV_LORA = 128
MLA_NOPE = 64
MLA_ROPE = 32
MLA_V = 64
DIFF_HEADS = 8
DIFF_HEAD_DIM = 32
DIFF_ROT = DIFF_HEAD_DIM // 4
DIFF_QK = DIFF_HEADS * 2 * DIFF_HEAD_DIM
DIFF_V = DIFF_HEADS * 2 * DIFF_HEAD_DIM
SSM_GROUP = 16
SSM_GROUPS = 32
SSM_WIDTH = SSM_GROUP * SSM_GROUPS
SSM_STATE = 64
N_BRANCH = 3
IN_WIDTH = MLA_Q_LORA + MLA_KV_LORA + MLA_ROPE + 2 * DIFF_QK + DIFF_V + SSM_WIDTH + N_BRANCH * D_MODEL
ROPE_THETA = 500000.0
Q_BLOCK = 128
N_EXPERTS = 32
N_GROUPS = 8
EXPERTS_PER_GROUP = N_EXPERTS // N_GROUPS
TOP_K = 2
EXPERT_FF = 512
MOE_BLOCK = 128
DN_ALPHA = (2 * DEPTH) ** 0.25
DN_BETA = (8 * DEPTH) ** -0.25
LN_EPS = 1e-5
RMS_EPS = 1e-6

kernel_name = "hybrid_mla_diff_s5_moe_deepnorm"


def _rmsnorm(x, g, eps=RMS_EPS):
    xf = x.astype(jnp.float32)
    y = xf * lax.rsqrt(jnp.mean(xf * xf, axis=-1, keepdims=True) + eps)
    return (y * g.astype(jnp.float32)).astype(x.dtype)


def _layernorm(x, g, b):
    xf = x.astype(jnp.float32)
    mu = jnp.mean(xf, axis=-1, keepdims=True)
    xc = xf - mu
    var = jnp.mean(xc * xc, axis=-1, keepdims=True)
    y = xc * lax.rsqrt(var + LN_EPS) * g.astype(jnp.float32) + b.astype(jnp.float32)
    return y.astype(x.dtype)


def _rope_tables(positions, rot_dim):
    inv = ROPE_THETA ** (-jnp.arange(0, rot_dim, 2, dtype=jnp.float32) / rot_dim)
    ang = positions.astype(jnp.float32)[..., None] * inv
    return jnp.cos(ang)[:, :, None, :], jnp.sin(ang)[:, :, None, :]


def _apply_rope(x, cos, sin):
    half = cos.shape[-1]
    rot = 2 * half
    xf = x[..., :rot].astype(jnp.float32)
    x1, x2 = xf[..., :half], xf[..., half:]
    r = jnp.concatenate([x1 * cos - x2 * sin, x2 * cos + x1 * sin], axis=-1).astype(x.dtype)
    return jnp.concatenate([r, x[..., rot:]], axis=-1)


def _causal_attention(q, k, v, scale):
    B, S, H, dq = q.shape
    nb = S // Q_BLOCK
    qb = jnp.moveaxis(q.reshape(B, nb, Q_BLOCK, H, dq), 1, 0)
    kpos = jnp.arange(S)

    def block(args):
        qi, bi = args
        s = jnp.einsum('bqhd,bkhd->bhqk', qi, k, preferred_element_type=jnp.float32) * scale
        qpos = bi * Q_BLOCK + jnp.arange(Q_BLOCK)
        s = jnp.where(kpos[None, :] <= qpos[:, None], s, -1e30)
        p = jax.nn.softmax(s, axis=-1).astype(v.dtype)
        return jnp.einsum('bhqk,bkhd->bqhd', p, v)

    o = lax.map(block, (qb, jnp.arange(nb)))
    return jnp.moveaxis(o, 0, 1).reshape(B, S, H, v.shape[-1])


def _split_in(u):
    sizes = (MLA_Q_LORA, MLA_KV_LORA, MLA_ROPE, DIFF_QK, DIFF_QK, DIFF_V, SSM_WIDTH, N_BRANCH * D_MODEL)
    cuts = [int(c) for c in np.cumsum(sizes)[:-1]]
    return jnp.split(u, cuts, axis=-1)


def _mla_branch(c_q, c_kv, k_r, cos, sin, q_norm, w_uq, kv_norm, w_ukv, w_out):
    B, S, _ = c_q.shape
    q = (_rmsnorm(c_q, q_norm) @ w_uq).reshape(B, S, MLA_HEADS, MLA_NOPE + MLA_ROPE)
    q = jnp.concatenate([q[..., :MLA_NOPE], _apply_rope(q[..., MLA_NOPE:], cos, sin)], axis=-1)
    kv = (_rmsnorm(c_kv, kv_norm) @ w_ukv).reshape(B, S, MLA_HEADS, MLA_NOPE + MLA_V)
    k_nope, v = kv[..., :MLA_NOPE], kv[..., MLA_NOPE:]
    k_rope = _apply_rope(k_r[:, :, None, :], cos, sin)
    k = jnp.concatenate([k_nope, jnp.broadcast_to(k_rope, (B, S, MLA_HEADS, MLA_ROPE))], axis=-1)
    o = _causal_attention(q, k, v, (MLA_NOPE + MLA_ROPE) ** -0.5)
    return o.reshape(B, S, MLA_HEADS * MLA_V) @ w_out


def _diff_branch(q, k, v, cos, sin, lam, subln, w_out, lambda_init):
    B, S, _ = q.shape
    q = q.reshape(B, S, DIFF_HEADS, 2, DIFF_HEAD_DIM)
    k = k.reshape(B, S, DIFF_HEADS, 2, DIFF_HEAD_DIM)
    v = v.reshape(B, S, DIFF_HEADS, 2 * DIFF_HEAD_DIM)
    q1, q2 = _apply_rope(q[:, :, :, 0], cos, sin), _apply_rope(q[:, :, :, 1], cos, sin)
    k1, k2 = _apply_rope(k[:, :, :, 0], cos, sin), _apply_rope(k[:, :, :, 1], cos, sin)
    lf = lam.astype(jnp.float32)
    lam_val = jnp.exp(jnp.sum(lf[0] * lf[1])) - jnp.exp(jnp.sum(lf[2] * lf[3])) + lambda_init
    scale = DIFF_HEAD_DIM ** -0.5
    o = _causal_attention(q1, k1, v, scale) - lam_val.astype(v.dtype) * _causal_attention(q2, k2, v, scale)
    o = _rmsnorm(o, subln, 1e-5) * (1.0 - lambda_init)
    return o.reshape(B, S, DIFF_V) @ w_out


def _ssm_branch(u, a_re, a_im, log_dt, b_re, b_im, c_re, c_im, d, w_glu):
    B, S, _ = u.shape
    uf = u.astype(jnp.float32).reshape(B, S, SSM_GROUPS, SSM_GROUP)
    A = lax.complex(a_re.astype(jnp.float32), a_im.astype(jnp.float32))
    dt = jnp.exp(log_dt.astype(jnp.float32))[:, None]
    a_bar = jnp.exp(A * dt)
    b_bar = ((a_bar - 1.0) / A)[..., None] * lax.complex(b_re.astype(jnp.float32), b_im.astype(jnp.float32))
    bu = jnp.einsum('bsgp,gnp->sbgn', uf.astype(jnp.complex64), b_bar)
    a_seq = jnp.broadcast_to(a_bar[None, None], (S, 1, SSM_GROUPS, SSM_STATE))

    def combine(left, right):
        a_l, b_l = left
        a_r, b_r = right
        return a_r * a_l, a_r * b_l + b_r

    _, h = lax.associative_scan(combine, (a_seq, bu), axis=0)
    C = lax.complex(c_re.astype(jnp.float32), c_im.astype(jnp.float32))
    y = jnp.real(jnp.einsum('sbgn,gpn->bsgp', h, C)) + d.astype(jnp.float32).reshape(SSM_GROUPS, SSM_GROUP) * uf
    y = jax.nn.gelu(y.reshape(B, S, SSM_WIDTH)).astype(u.dtype)
    gl = y @ w_glu
    return gl[..., :D_MODEL] * jax.nn.sigmoid(gl[..., D_MODEL:])


def _moe(x, router_w, router_bias, w_gate, w_up, w_down):
    B, S, D = x.shape
    T = B * S
    xt = x.reshape(T, D)
    scores = jax.nn.sigmoid((xt @ router_w).astype(jnp.float32))
    biased = scores + router_bias.astype(jnp.float32)
    grouped = biased.reshape(T, N_GROUPS, EXPERTS_PER_GROUP)
    group_score = jnp.sum(lax.top_k(grouped, TOP_K)[0], axis=-1)
    g_sel = jnp.argmax(group_score, axis=-1)
    in_group = jnp.take_along_axis(grouped, g_sel[:, None, None], axis=1)[:, 0]
    _, loc = lax.top_k(in_group, TOP_K)
    idx = g_sel[:, None] * EXPERTS_PER_GROUP + loc
    w = jnp.take_along_axis(scores, idx, axis=1)
    w = w / jnp.sum(w, axis=-1, keepdims=True)
    A = T * TOP_K
    e_flat = idx.reshape(A)
    tok_flat = jnp.arange(A, dtype=jnp.int32) // TOP_K
    order = jnp.argsort(e_flat)
    e_sorted, tok_sorted, w_sorted = e_flat[order], tok_flat[order], w.reshape(A)[order]
    counts = jnp.zeros((N_EXPERTS,), jnp.int32).at[e_flat].add(1)
    starts = jnp.cumsum(counts) - counts
    padded = (counts + MOE_BLOCK - 1) // MOE_BLOCK * MOE_BLOCK
    pad_ends = jnp.cumsum(padded)
    pad_starts = pad_ends - padded
    dest = pad_starts[e_sorted] + jnp.arange(A, dtype=jnp.int32) - starts[e_sorted]
    n_blocks = -(-A // MOE_BLOCK) + N_EXPERTS
    R = n_blocks * MOE_BLOCK
    row_tok = jnp.full((R,), T, jnp.int32).at[dest].set(tok_sorted)
    row_w = jnp.zeros((R,), jnp.float32).at[dest].set(w_sorted)
    block_e = jnp.minimum(jnp.searchsorted(pad_ends, jnp.arange(n_blocks, dtype=jnp.int32) * MOE_BLOCK, side='right'), N_EXPERTS - 1)
    x_pad = jnp.concatenate([xt, jnp.zeros((1, D), xt.dtype)], axis=0)
    xb = x_pad[row_tok].reshape(n_blocks, MOE_BLOCK, D)

    def expert_block(args):
        xi, e = args
        hdn = jax.nn.silu(xi @ w_gate[e]) * (xi @ w_up[e])
        return hdn @ w_down[e]

    yb = lax.map(expert_block, (xb, block_e)).reshape(R, D)
    y = jnp.zeros((T + 1, D), jnp.float32).at[row_tok].add(yb.astype(jnp.float32) * row_w[:, None])[:T]
    return y.astype(x.dtype).reshape(B, S, D)


def setup_inputs(seed: int = 0) -> dict:
    key = jax.random.key(seed)
    keys = jax.random.split(key, 40)
    counter = [0]

    def nxt():
        counter[0] += 1
        return keys[counter[0] - 1]

    def nrm(shape, scale):
        return jax.random.normal(nxt(), shape, jnp.float32) * scale

    L = DEPTH
    x = nrm((BATCH, SEQ, D_MODEL), 1.0)
    positions = (jax.random.randint(nxt(), (BATCH, 1), 0, 1024) + jnp.arange(SEQ)[None, :]).astype(jnp.int32)
    w_in = nrm((L, D_MODEL, IN_WIDTH), D_MODEL ** -0.5)
    b_gate = nrm((L, N_BRANCH, D_MODEL), 0.01)
    mla_q_norm = 1.0 + nrm((L, MLA_Q_LORA), 0.01)
    mla_w_uq = nrm((L, MLA_Q_LORA, MLA_HEADS * (MLA_NOPE + MLA_ROPE)), MLA_Q_LORA ** -0.5)
    mla_kv_norm = 1.0 + nrm((L, MLA_KV_LORA), 0.01)
    mla_w_ukv = nrm((L, MLA_KV_LORA, MLA_HEADS * (MLA_NOPE + MLA_V)), MLA_KV_LORA ** -0.5)
    mla_w_out = nrm((L, MLA_HEADS * MLA_V, D_MODEL), (MLA_HEADS * MLA_V) ** -0.5)
    diff_lambda = nrm((L, 4, DIFF_HEAD_DIM), 0.1)
    diff_subln = 1.0 + nrm((L, 2 * DIFF_HEAD_DIM), 0.01)
    diff_w_out = nrm((L, DIFF_V, D_MODEL), DIFF_V ** -0.5)
    ssm_a_re = -0.5 + nrm((L, SSM_GROUPS, SSM_STATE), 0.01)
    ssm_a_im = jnp.pi * jnp.arange(SSM_STATE, dtype=jnp.float32) + nrm((L, SSM_GROUPS, SSM_STATE), 0.01)
    ssm_log_dt = jax.random.uniform(nxt(), (L, SSM_GROUPS), jnp.float32, math.log(1e-3), math.log(1e-1))
    ssm_b_re = nrm((L, SSM_GROUPS, SSM_STATE, SSM_GROUP), (2 * SSM_GROUP) ** -0.5)
    ssm_b_im = nrm((L, SSM_GROUPS, SSM_STATE, SSM_GROUP), (2 * SSM_GROUP) ** -0.5)
    ssm_c_re = nrm((L, SSM_GROUPS, SSM_GROUP, SSM_STATE), (2 * SSM_STATE) ** -0.5)
    ssm_c_im = nrm((L, SSM_GROUPS, SSM_GROUP, SSM_STATE), (2 * SSM_STATE) ** -0.5)
    ssm_d = nrm((L, SSM_WIDTH), 1.0)
    ssm_w_glu = nrm((L, SSM_WIDTH, 2 * D_MODEL), SSM_WIDTH ** -0.5)
    w_o = nrm((L, D_MODEL, D_MODEL), D_MODEL ** -0.5 * DN_BETA)
    ln_gain = 1.0 + nrm((L, 2, D_MODEL), 0.01)
    ln_bias = nrm((L, 2, D_MODEL), 0.01)
    router_w = nrm((D_MODEL, N_EXPERTS), D_MODEL ** -0.5)
    router_bias = nrm((N_EXPERTS,), 0.01)
    moe_w_gate = nrm((L, N_EXPERTS, D_MODEL, EXPERT_FF), D_MODEL ** -0.5)
    moe_w_up = nrm((L, N_EXPERTS, D_MODEL, EXPERT_FF), D_MODEL ** -0.5)
    moe_w_down = nrm((L, N_EXPERTS, EXPERT_FF, D_MODEL), EXPERT_FF ** -0.5 * DN_BETA)
    return {"x": x, "positions": positions, "w_in": w_in, "b_gate": b_gate,
            "mla_q_norm": mla_q_norm, "mla_w_uq": mla_w_uq, "mla_kv_norm": mla_kv_norm,
            "mla_w_ukv": mla_w_ukv, "mla_w_out": mla_w_out, "diff_lambda": diff_lambda,
            "diff_subln": diff_subln, "diff_w_out": diff_w_out, "ssm_a_re": ssm_a_re,
            "ssm_a_im": ssm_a_im, "ssm_log_dt": ssm_log_dt, "ssm_b_re": ssm_b_re,
            "ssm_b_im": ssm_b_im, "ssm_c_re": ssm_c_re, "ssm_c_im": ssm_c_im, "ssm_d": ssm_d,
            "ssm_w_glu": ssm_w_glu, "w_o": w_o, "ln_gain": ln_gain, "ln_bias": ln_bias,
            "router_w": router_w, "router_bias": router_bias, "moe_w_gate": moe_w_gate,
            "moe_w_up": moe_w_up, "moe_w_down": moe_w_down}


def reference(x, positions, w_in, b_gate, mla_q_norm, mla_w_uq, mla_kv_norm, mla_w_ukv, mla_w_out,
              diff_lambda, diff_subln, diff_w_out, ssm_a_re, ssm_a_im, ssm_log_dt, ssm_b_re, ssm_b_im,
              ssm_c_re, ssm_c_im, ssm_d, ssm_w_glu, w_o, ln_gain, ln_bias, router_w, router_bias,
              moe_w_gate, moe_w_up, moe_w_down):
    B, S, _ = x.shape
    cos_a, sin_a = _rope_tables(positions, MLA_ROPE)
    cos_b, sin_b = _rope_tables(positions, DIFF_ROT)
    for l in range(DEPTH):
        u = x @ w_in[l]
        c_q, c_kv, k_r, q_d, k_d, v_d, u_s, g_logit = _split_in(u)
        y_a = _mla_branch(c_q, c_kv, k_r, cos_a, sin_a, mla_q_norm[l], mla_w_uq[l],
                          mla_kv_norm[l], mla_w_ukv[l], mla_w_out[l])
        lambda_init = 0.8 - 0.6 * math.exp(-0.3 * l)
        y_b = _diff_branch(q_d, k_d, v_d, cos_b, sin_b, diff_lambda[l], diff_subln[l],
                           diff_w_out[l], lambda_init)
        y_c = _ssm_branch(u_s, ssm_a_re[l], ssm_a_im[l], ssm_log_dt[l], ssm_b_re[l], ssm_b_im[l],
                          ssm_c_re[l], ssm_c_im[l], ssm_d[l], ssm_w_glu[l])
        gates = jax.nn.sigmoid((g_logit.reshape(B, S, N_BRANCH, D_MODEL) + b_gate[l]).astype(jnp.float32)).astype(x.dtype)
        merged = gates[:, :, 0] * y_a + gates[:, :, 1] * y_b + gates[:, :, 2] * y_c
        x = _layernorm(DN_ALPHA * x + merged @ w_o[l], ln_gain[l, 0], ln_bias[l, 0])
        x = _layernorm(DN_ALPHA * x + _moe(x, router_w, router_bias, moe_w_gate[l], moe_w_up[l], moe_w_down[l]),
                       ln_gain[l, 1], ln_bias[l, 1])
    return x
```

```python
import functools
import math

import jax
import jax.numpy as jnp
from jax import lax
from jax.experimental import pallas as pl
from jax.experimental.pallas import tpu as pltpu

MLA_HEADS = 8
MLA_Q_LORA = 256
MLA_KV_LORA = 128
MLA_NOPE = 64
MLA_ROPE = 32
MLA_V = 64
DIFF_HEADS = 8
DIFF_HEAD_DIM = 32
DIFF_ROT = DIFF_HEAD_DIM // 4
SSM_GROUP = 16
SSM_STATE = 64
N_BRANCH = 3
ROPE_THETA = 500000.0
N_GROUPS = 8
TOP_K = 2
LN_EPS = 1e-5
RMS_EPS = 1e-6
SUBLN_EPS = 1e-5

LANE = 128
ROW_TILE = 256
ATTN_TILE = 512
SSM_CHUNK = 16
MOE_ROWS = 256
VMEM_LIMIT = 56 * 1024 * 1024

NEG_BIG = -1e30
LOG2E = math.log2(math.e)

_F32 = jnp.float32
_BF16 = jnp.bfloat16


def _cparams(n_axes):
    return pltpu.CompilerParams(dimension_semantics=("arbitrary",) * n_axes,
                                vmem_limit_bytes=VMEM_LIMIT)


def _dot(a, b):
    return jnp.dot(a, b, preferred_element_type=_F32)


def _dot_nt(a, b, precision=None):
    return lax.dot_general(a, b, (((1,), (1,)), ((), ())), precision=precision,
                           preferred_element_type=_F32)


def _const_spec(shape):
    zeros = (0,) * len(shape)
    return pl.BlockSpec(shape, lambda *_: zeros)


def _rope_table_kernel(pos_ref, inv_ref, sel_ref, out_ref):
    pos = pos_ref[...].astype(_F32)
    ang = pos * inv_ref[...]
    c, s = jnp.cos(ang), jnp.sin(ang)
    sel = sel_ref[...]
    out_ref[0] = c * (sel[0:1] + sel[1:2]) + sel[2:3]
    out_ref[1] = -s * sel[0:1]
    out_ref[2] = s * sel[1:2]


def _rope_tables(pos_col, inv_row, sel_rows, tm):
    t = pos_col.shape[0]
    return pl.pallas_call(
        _rope_table_kernel,
        grid=(t // tm,),
        in_specs=[pl.BlockSpec((tm, 1), lambda i: (i, 0)), _const_spec((1, LANE)), _const_spec((3, LANE))],
        out_specs=pl.BlockSpec((3, tm, LANE), lambda i: (0, i, 0)),
        out_shape=jax.ShapeDtypeStruct((3, t, LANE), _F32),
        compiler_params=_cparams(1),
        name="rope_tables",
    )(pos_col, inv_row, sel_rows)


def _rope_lane_patterns():
    lane = jnp.arange(LANE)
    half_a = MLA_ROPE // 2
    in_rope = (lane >= MLA_NOPE) & (lane < MLA_NOPE + MLA_ROPE)
    fi = (lane - MLA_NOPE) % half_a
    inv_a = jnp.where(in_rope, ROPE_THETA ** (-(2.0 * fi.astype(_F32)) / MLA_ROPE), 0.0)
    x1_a = in_rope & (lane < MLA_NOPE + half_a)
    x2_a = in_rope & ~x1_a
    sel_a = jnp.stack([x1_a, x2_a, ~in_rope]).astype(_F32)
    half_d = DIFF_ROT // 2
    d = lane % DIFF_HEAD_DIM
    in_rot = d < DIFF_ROT
    inv_d = jnp.where(in_rot, ROPE_THETA ** (-(2.0 * (d % half_d).astype(_F32)) / DIFF_ROT), 0.0)
    x1_d = d < half_d
    x2_d = in_rot & ~x1_d
    sel_d = jnp.stack([x1_d, x2_d, ~in_rot]).astype(_F32)
    return (inv_a[None].astype(_F32), sel_a), (inv_d[None].astype(_F32), sel_d)


def _rope_chunk(x, tab_ref, half):
    return (x * tab_ref[0] + pltpu.roll(x, LANE - half, 1) * tab_ref[1]
            + pltpu.roll(x, half, 1) * tab_ref[2])


def _rms(x, g, eps):
    return x * lax.rsqrt(jnp.mean(x * x, axis=-1, keepdims=True) + eps) * g


def _inproj_kernel(x_ref, taba_ref, tabd_ref, w1_ref, qn_ref, wuq_ref, kvn_ref, wk_ref, wv_ref, bg_ref,
                   qm_ref, km_ref, vm_ref, qd_ref, kd_ref, vd_ref, us_ref, gates_ref,
                   *, q_scale_mla, q_scale_diff, d_model):
    xb = x_ref[...].astype(_BF16)
    n_mla = MLA_HEADS
    lat = _dot(xb, w1_ref[:, 0:4 * LANE])
    cqn = _rms(lat[:, :MLA_Q_LORA], qn_ref[...], RMS_EPS).astype(_BF16)
    q = _dot(cqn, wuq_ref[...])
    for h in range(n_mla):
        sl = slice(h * LANE, (h + 1) * LANE)
        qm_ref[:, sl] = (_rope_chunk(q[:, sl], taba_ref, MLA_ROPE // 2) * q_scale_mla).astype(_BF16)
    ckvn = _rms(lat[:, MLA_Q_LORA:MLA_Q_LORA + MLA_KV_LORA], kvn_ref[...], RMS_EPS).astype(_BF16)
    kn = _dot(ckvn, wk_ref[...])
    kr = _rope_chunk(lat[:, 3 * LANE:4 * LANE], taba_ref, MLA_ROPE // 2)
    for h in range(n_mla):
        sl = slice(h * LANE, (h + 1) * LANE)
        km_ref[:, sl] = (kn[:, sl] + kr).astype(_BF16)
    vm_ref[...] = _dot(ckvn, wv_ref[...]).astype(_BF16)

    seg = DIFF_HEADS * 2 * DIFF_HEAD_DIM
    base = 4 * LANE
    lane = lax.broadcasted_iota(jnp.int32, (1, LANE), 1)
    comp = lane // DIFF_HEAD_DIM
    qd = _dot(xb, w1_ref[:, base:base + seg])
    kd = _dot(xb, w1_ref[:, base + seg:base + 2 * seg])
    for c in range(seg // LANE):
        sl = slice(c * LANE, (c + 1) * LANE)
        qd_ref[:, sl] = (_rope_chunk(qd[:, sl], tabd_ref, DIFF_ROT // 2) * q_scale_diff).astype(_BF16)
        kc = _rope_chunk(kd[:, sl], tabd_ref, DIFF_ROT // 2)
        for v in range(4):
            kd_ref[v, :, sl] = jnp.where(comp == v, kc, 0.0).astype(_BF16)
    vd_ref[...] = _dot(xb, w1_ref[:, base + 2 * seg:base + 3 * seg]).astype(_BF16)
    us_ref[...] = _dot(xb, w1_ref[:, base + 3 * seg:base + 4 * seg]).astype(_BF16)
    gbase = base + 4 * seg
    for c in range(N_BRANCH):
        sl = slice(c * d_model, (c + 1) * d_model)
        g = _dot(xb, w1_ref[:, gbase + c * d_model:gbase + (c + 1) * d_model]) + bg_ref[:, sl]
        gates_ref[:, sl] = jax.nn.sigmoid(g).astype(_BF16)


def _inproj(x2d, taba, tabd, w1, qn, wuq, kvn, wk, wv, bg, tm):
    t, d_model = x2d.shape
    seg = DIFF_HEADS * 2 * DIFF_HEAD_DIM
    hm = MLA_HEADS * LANE
    row = lambda w: pl.BlockSpec((tm, w), lambda i: (i, 0))
    tab = pl.BlockSpec((3, tm, LANE), lambda i: (0, i, 0))
    kern = functools.partial(
        _inproj_kernel,
        q_scale_mla=float((MLA_NOPE + MLA_ROPE) ** -0.5 * LOG2E),
        q_scale_diff=float(DIFF_HEAD_DIM ** -0.5 * LOG2E),
        d_model=d_model)
    out_shapes = (
        jax.ShapeDtypeStruct((t, hm), _BF16), jax.ShapeDtypeStruct((t, hm), _BF16),
        jax.ShapeDtypeStruct((t, MLA_HEADS * MLA_V), _BF16),
        jax.ShapeDtypeStruct((t, seg), _BF16), jax.ShapeDtypeStruct((4, t, seg), _BF16),
        jax.ShapeDtypeStruct((t, seg), _BF16), jax.ShapeDtypeStruct((t, seg), _BF16),
        jax.ShapeDtypeStruct((t, N_BRANCH * d_model), _BF16))
    out_specs = (row(hm), row(hm), row(MLA_HEADS * MLA_V), row(seg),
                 pl.BlockSpec((4, tm, seg), lambda i: (0, i, 0)), row(seg), row(seg), row(N_BRANCH * d_model))
    return pl.pallas_call(
        kern,
        grid=(t // tm,),
        in_specs=[row(d_model), tab, tab, _const_spec(w1.shape), _const_spec(qn.shape), _const_spec(wuq.shape),
                  _const_spec(kvn.shape), _const_spec(wk.shape), _const_spec(wv.shape), _const_spec(bg.shape)],
        out_specs=out_specs,
        out_shape=out_shapes,
        compiler_params=_cparams(1),
        name="inproj",
    )(x2d, taba, tabd, w1, qn, wuq, kvn, wk, wv, bg)


def _flash_map(q, load_k, load_v, qi, tq):
    def step(start, carry, diagonal):
        m, l, acc = carry
        s = _dot_nt(q, load_k(start))
        if diagonal:
            r = lax.broadcasted_iota(jnp.int32, s.shape, 0)
            c = lax.broadcasted_iota(jnp.int32, s.shape, 1)
            s = jnp.where(c <= r, s, NEG_BIG)
        m_new = jnp.maximum(m, jnp.max(s, axis=-1, keepdims=True))
        alpha = jnp.exp2(m - m_new)
        p = jnp.exp2(s - m_new)
        l = alpha * l + jnp.sum(p, axis=-1, keepdims=True)
        acc = alpha * acc + _dot(p.astype(_BF16), load_v(start))
        return m_new, l, acc

    init = (jnp.full((tq, 1), NEG_BIG, _F32), jnp.zeros((tq, 1), _F32), jnp.zeros((tq, LANE), _F32))
    carry = lax.fori_loop(0, qi, lambda j, c: step(pl.multiple_of(j * tq, tq), c, False), init)
    _, l, acc = step(pl.multiple_of(qi * tq, tq), carry, True)
    return acc / l


def _mla_attn_kernel(q_ref, k_ref, v_ref, o_ref, *, tq):
    qi = pl.program_id(2)
    outs = []
    for hh in range(2):
        sl = slice(hh * LANE, (hh + 1) * LANE)
        outs.append(_flash_map(q_ref[0, :, sl],
                               lambda st, sl=sl: k_ref[0, pl.ds(st, tq), sl],
                               lambda st: v_ref[0, pl.ds(st, tq), :], qi, tq))
    lane = lax.broadcasted_iota(jnp.int32, (tq, LANE), 1)
    o_ref[0] = jnp.where(lane < MLA_V, outs[0], outs[1]).astype(o_ref.dtype)


def _mla_attention(q, k, v, tq):
    b, s, _ = q.shape
    pairs = MLA_HEADS // 2
    return pl.pallas_call(
        functools.partial(_mla_attn_kernel, tq=tq),
        grid=(b, pairs, s // tq),
        in_specs=[pl.BlockSpec((1, tq, 2 * LANE), lambda bi, p, qi: (bi, qi, p)),
                  pl.BlockSpec((1, s, 2 * LANE), lambda bi, p, qi: (bi, 0, p)),
                  pl.BlockSpec((1, s, LANE), lambda bi, p, qi: (bi, 0, p))],
        out_specs=pl.BlockSpec((1, tq, LANE), lambda bi, p, qi: (bi, qi, p)),
        out_shape=jax.ShapeDtypeStruct((b, s, pairs * LANE), _BF16),
        compiler_params=_cparams(3),
        name="mla_attention",
    )(q, k, v)


def _diff_attn_kernel(q_ref, k_ref, v_ref, lam_ref, subln_ref, o_ref, *, tq, lambda_init):
    qi = pl.program_id(2)
    q = q_ref[0]
    outs = [_flash_map(q, lambda st, v=v: k_ref[v, 0, pl.ds(st, tq), :],
                       lambda st: v_ref[0, pl.ds(st, tq), :], qi, tq) for v in range(4)]
    lf = lam_ref[...]
    lam = (jnp.exp(jnp.sum(lf[0:1] * lf[1:2], axis=-1, keepdims=True))
           - jnp.exp(jnp.sum(lf[2:3] * lf[3:4], axis=-1, keepdims=True)) + lambda_init)
    lane = lax.broadcasted_iota(jnp.int32, (tq, LANE), 1)
    first = lane < 2 * DIFF_HEAD_DIM
    d = jnp.where(first, outs[0] - lam * outs[1], outs[2] - lam * outs[3])
    sq = d * d
    ss_a = jnp.sum(jnp.where(first, sq, 0.0), axis=-1, keepdims=True)
    ss_b = jnp.sum(jnp.where(first, 0.0, sq), axis=-1, keepdims=True)
    ms = jnp.where(first, ss_a, ss_b) * (1.0 / (2 * DIFF_HEAD_DIM))
    o_ref[0] = (d * lax.rsqrt(ms + SUBLN_EPS) * subln_ref[...] * (1.0 - lambda_init)).astype(o_ref.dtype)


def _diff_attention(q, kvar, v, lam, subln2, tq, lambda_init):
    b, s, _ = q.shape
    pairs = DIFF_HEADS // 2
    return pl.pallas_call(
        functools.partial(_diff_attn_kernel, tq=tq, lambda_init=lambda_init),
        grid=(b, pairs, s // tq),
        in_specs=[pl.BlockSpec((1, tq, LANE), lambda bi, p, qi: (bi, qi, p)),
                  pl.BlockSpec((4, 1, s, LANE), lambda bi, p, qi: (0, bi, 0, p)),
                  pl.BlockSpec((1, s, LANE), lambda bi, p, qi: (bi, 0, p)),
                  _const_spec(lam.shape), _const_spec(subln2.shape)],
        out_specs=pl.BlockSpec((1, tq, LANE), lambda bi, p, qi: (bi, qi, p)),
        out_shape=jax.ShapeDtypeStruct((b, s, pairs * LANE), _BF16),
        compiler_params=_cparams(3),
        name="diff_attention",
    )(q, kvar, v, lam, subln2)


def _ssm_kernel(u_ref, t_ref, b_ref, c_ref, ar_ref, ai_ref, d_ref, y_ref, x_scr, h_scr, *, batch, chunks):
    u = u_ref[0]
    x_scr[...] = _dot(u, b_ref[0])
    ar, ai = ar_ref[0], ai_ref[0]

    def step(c, h):
        r0 = c * batch
        if batch % 8 == 0:
            r0 = pl.multiple_of(r0, 8)
        h_scr[pl.ds(r0, batch), :] = h
        return ar * h + ai * pltpu.roll(h, SSM_STATE, 1) + x_scr[pl.ds(r0, batch), :]

    lax.fori_loop(0, chunks, step, jnp.zeros((batch, 2 * SSM_STATE), _F32))
    y = _dot(u, t_ref[0]) + _dot(h_scr[...].astype(_BF16), c_ref[0]) + u.astype(_F32) * d_ref[0]
    y_ref[0] = jax.nn.gelu(y).astype(y_ref.dtype)


def _ssm(u_g, tmat, bmat, cmat, ar, ai, dflat, batch):
    g, r, w = u_g.shape
    per_g = lambda shape: pl.BlockSpec((1,) + shape, lambda i: (i, 0, 0))
    return pl.pallas_call(
        functools.partial(_ssm_kernel, batch=batch, chunks=r // batch),
        grid=(g,),
        in_specs=[per_g((r, w)), per_g((w, w)), per_g((w, 2 * SSM_STATE)), per_g((2 * SSM_STATE, w)),
                  per_g((1, 2 * SSM_STATE)), per_g((1, 2 * SSM_STATE)), per_g((1, w))],
        out_specs=per_g((r, w)),
        out_shape=jax.ShapeDtypeStruct((g, r, w), _BF16),
        scratch_shapes=[pltpu.VMEM((r, 2 * SSM_STATE), _F32), pltpu.VMEM((r, 2 * SSM_STATE), _F32)],
        compiler_params=_cparams(1),
        name="ssm_scan",
    )(u_g, tmat, bmat, cmat, ar, ai, dflat)


def _ssm_matrices(a_re, a_im, log_dt, b_re, b_im, c_re, c_im, d):
    L, P, N = SSM_CHUNK, SSM_GROUP, SSM_STATE
    G = a_re.shape[0]
    A = lax.complex(a_re.astype(_F32), a_im.astype(_F32))
    dt = jnp.exp(log_dt.astype(_F32))[:, None]
    a_bar = jnp.exp(A * dt)
    b_bar = ((a_bar - 1.0) / A)[..., None] * lax.complex(b_re.astype(_F32), b_im.astype(_F32))
    C = lax.complex(c_re.astype(_F32), c_im.astype(_F32))
    steps = jnp.arange(L + 1, dtype=_F32)
    apow = jnp.exp((A * dt)[None] * steps[:, None, None])
    m = jnp.real(jnp.einsum('gpn,dgn,gnq->dgpq', C, apow[:L], b_bar))
    s_idx = jnp.arange(L)[:, None]
    t_idx = jnp.arange(L)[None, :]
    lag = jnp.clip(t_idx - s_idx, 0, L - 1)
    tfull = jnp.where((t_idx >= s_idx)[:, :, None, None, None], m[lag], 0.0)
    tmat = tfull.transpose(2, 0, 4, 1, 3).reshape(G, L * P, L * P)
    bm = apow[L - 1 - jnp.arange(L)][:, :, :, None] * b_bar[None]
    bm = bm.transpose(1, 0, 3, 2).reshape(G, L * P, N)
    bmat = jnp.concatenate([jnp.real(bm), jnp.imag(bm)], axis=-1)
    cm = C[None] * apow[1:L + 1][:, :, None, :]
    cm = cm.transpose(1, 3, 0, 2).reshape(G, N, L * P)
    cmat = jnp.concatenate([jnp.real(cm), -jnp.imag(cm)], axis=1)
    al = apow[L]
    ar = jnp.concatenate([jnp.real(al), jnp.real(al)], axis=-1)[:, None, :]
    ai = jnp.concatenate([-jnp.imag(al), jnp.imag(al)], axis=-1)[:, None, :]
    dflat = jnp.tile(d.astype(_F32).reshape(G, 1, P), (1, L, 1)).reshape(G, 1, L * P)
    return tmat.astype(_BF16), bmat.astype(_BF16), cmat.astype(_BF16), ar, ai, dflat


def _layernorm(z, g, b):
    mu = jnp.mean(z, axis=-1, keepdims=True)
    zc = z - mu
    var = jnp.mean(zc * zc, axis=-1, keepdims=True)
    return zc * lax.rsqrt(var + LN_EPS) * g + b


def _merge_kernel(om_ref, od_ref, ys_ref, g_ref, x_ref, wmo_ref, wdo_ref, wglu_ref, wo_ref, lng_ref, lnb_ref,
                  rw_ref, rb_ref, x1_ref, idx_ref, wts_ref, *, alpha, d_model):
    ya = _dot(om_ref[...], wmo_ref[...])
    yb = _dot(od_ref[...], wdo_ref[...])
    gl = _dot(ys_ref[...], wglu_ref[...])
    yc = gl[:, :d_model] * jax.nn.sigmoid(gl[:, d_model:])
    merged = (g_ref[:, 0:d_model].astype(_F32) * ya + g_ref[:, d_model:2 * d_model].astype(_F32) * yb
              + g_ref[:, 2 * d_model:3 * d_model].astype(_F32) * yc)
    z = alpha * x_ref[...] + _dot(merged.astype(_BF16), wo_ref[...])
    x1 = _layernorm(z, lng_ref[...], lnb_ref[...])
    x1_ref[...] = x1

    logits = _dot_nt(rw_ref[...], x1, precision=lax.Precision.HIGHEST)
    scores = jax.nn.sigmoid(logits)
    biased = scores + rb_ref[...]
    epg = logits.shape[0] // N_GROUPS
    a = [biased[j * N_GROUPS:(j + 1) * N_GROUPS] for j in range(epg)]
    sc = [scores[j * N_GROUPS:(j + 1) * N_GROUPS] for j in range(epg)]
    gs = None
    for i in range(epg):
        for j in range(i + 1, epg):
            pair = a[i] + a[j]
            gs = pair if gs is None else jnp.maximum(gs, pair)
    giota = lax.broadcasted_iota(jnp.int32, gs.shape, 0)
    gmax = jnp.max(gs, axis=0, keepdims=True)
    gsel = jnp.min(jnp.where(gs == gmax, giota, N_GROUPS), axis=0, keepdims=True)
    hot = giota == gsel
    val = [jnp.sum(jnp.where(hot, a[j], 0.0), axis=0, keepdims=True) for j in range(epg)]
    raw = [jnp.sum(jnp.where(hot, sc[j], 0.0), axis=0, keepdims=True) for j in range(epg)]
    b1, i1, w1 = val[0], jnp.zeros_like(gsel), raw[0]
    for j in range(1, epg):
        take = val[j] > b1
        b1 = jnp.where(take, val[j], b1)
        i1 = jnp.where(take, j, i1)
        w1 = jnp.where(take, raw[j], w1)
    b2 = jnp.full_like(b1, -jnp.inf)
    i2 = jnp.zeros_like(gsel)
    w2 = jnp.zeros_like(w1)
    for j in range(epg):
        take = (i1 != j) & (val[j] > b2)
        b2 = jnp.where(take, val[j], b2)
        i2 = jnp.where(take, j, i2)
        w2 = jnp.where(take, raw[j], w2)
    wsum = w1 + w2
    idx_ref[...] = jnp.concatenate([gsel * epg + i1, gsel * epg + i2], axis=0)
    wts_ref[...] = jnp.concatenate([w1 / wsum, w2 / wsum], axis=0)


def _merge(om, od, ys, gates, x2d, wmo, wdo, wglu, wo, lng, lnb, rw, rb, tm, alpha):
    t, d_model = x2d.shape
    row = lambda w: pl.BlockSpec((tm, w), lambda i: (i, 0))
    col = pl.BlockSpec((TOP_K, tm), lambda i: (0, i))
    return pl.pallas_call(
        functools.partial(_merge_kernel, alpha=alpha, d_model=d_model),
        grid=(t // tm,),
        in_specs=[row(om.shape[1]), row(od.shape[1]), row(ys.shape[1]), row(gates.shape[1]), row(d_model),
                  _const_spec(wmo.shape), _const_spec(wdo.shape), _const_spec(wglu.shape), _const_spec(wo.shape),
                  _const_spec(lng.shape), _const_spec(lnb.shape), _const_spec(rw.shape), _const_spec(rb.shape)],
        out_specs=(row(d_model), col, col),
        out_shape=(jax.ShapeDtypeStruct((t, d_model), _F32), jax.ShapeDtypeStruct((TOP_K, t), jnp.int32),
                   jax.ShapeDtypeStruct((TOP_K, t), _F32)),
        compiler_params=_cparams(1),
        name="merge_router",
    )(om, od, ys, gates, x2d, wmo, wdo, wglu, wo, lng, lnb, rw, rb)


def _moe_kernel(be_ref, cnt_ref, tok_ref, dst_ref, w_ref, x_hbm, wg_ref, wu_ref, wd_ref, out_hbm,
                xbuf, ybuf, sem_in, sem_out):
    i = pl.program_id(0)
    cnt = cnt_ref[i]

    def gather(r):
        return pltpu.make_async_copy(x_hbm.at[pl.ds(tok_ref[0, 0, r], 1), :], xbuf.at[pl.ds(r, 1), :], sem_in)

    def scatter(r):
        return pltpu.make_async_copy(ybuf.at[pl.ds(r, 1), :], out_hbm.at[pl.ds(dst_ref[0, 0, r], 1), :], sem_out)

    def for_rows(fn):
        def body(r, carry):
            fn(r)
            return carry
        lax.fori_loop(0, cnt, body, 0)

    @pl.when(cnt > 0)
    def _():
        xbuf[...] = jnp.zeros_like(xbuf)
        for_rows(lambda r: gather(r).start())
        for_rows(lambda r: gather(r).wait())
        xb = xbuf[...].astype(_BF16)
        h = jax.nn.silu(_dot(xb, wg_ref[0])) * _dot(xb, wu_ref[0])
        ybuf[...] = _dot(h.astype(_BF16), wd_ref[0]) * w_ref[...]
        for_rows(lambda r: scatter(r).start())
        for_rows(lambda r: scatter(r).wait())


def _moe(block_e, block_cnt, row_tok, row_dst, row_w, x1, wg, wu, wd):
    t, d_model = x1.shape
    n_blocks = block_e.shape[0]
    mb = row_w.shape[0] // n_blocks
    ff = wg.shape[-1]
    smem_row = pl.BlockSpec((1, 1, mb), lambda i, be, cnt: (i, 0, 0), memory_space=pltpu.SMEM)
    grid_spec = pltpu.PrefetchScalarGridSpec(
        num_scalar_prefetch=2,
        grid=(n_blocks,),
        in_specs=[smem_row, smem_row,
                  pl.BlockSpec((mb, 1), lambda i, be, cnt: (i, 0)),
                  pl.BlockSpec(memory_space=pl.ANY),
                  pl.BlockSpec((1, d_model, ff), lambda i, be, cnt: (be[i], 0, 0)),
                  pl.BlockSpec((1, d_model, ff), lambda i, be, cnt: (be[i], 0, 0)),
                  pl.BlockSpec((1, ff, d_model), lambda i, be, cnt: (be[i], 0, 0))],
        out_specs=pl.BlockSpec(memory_space=pl.ANY),
        scratch_shapes=[pltpu.VMEM((mb, d_model), _F32), pltpu.VMEM((mb, d_model), _F32),
                        pltpu.SemaphoreType.DMA(()), pltpu.SemaphoreType.DMA(())])
    return pl.pallas_call(
        _moe_kernel,
        grid_spec=grid_spec,
        out_shape=jax.ShapeDtypeStruct((TOP_K * t, d_model), _F32),
        compiler_params=_cparams(1),
        name="moe_experts",
    )(block_e, block_cnt, row_tok.reshape(n_blocks, 1, mb), row_dst.reshape(n_blocks, 1, mb),
      row_w.reshape(-1, 1), x1, wg, wu, wd)


def _moe_plan(idx, wts, n_experts, mb):
    k, t = idx.shape
    a_total = k * t
    e_flat = idx.reshape(a_total)
    order = jnp.argsort(e_flat).astype(jnp.int32)
    counts = jnp.sum((e_flat[:, None] == jnp.arange(n_experts)[None, :]).astype(jnp.int32), axis=0)
    starts = jnp.cumsum(counts) - counts
    padded = (counts + mb - 1) // mb * mb
    pad_ends = jnp.cumsum(padded)
    pad_starts = pad_ends - padded
    n_blocks = -(-a_total // mb) + n_experts
    blk_start = jnp.arange(n_blocks, dtype=jnp.int32) * mb
    block_e = jnp.minimum(jnp.searchsorted(pad_ends, blk_start, side='right'), n_experts - 1).astype(jnp.int32)
    block_cnt = jnp.clip(counts[block_e] - (blk_start - pad_starts[block_e]), 0, mb).astype(jnp.int32)
    r = jnp.arange(n_blocks * mb, dtype=jnp.int32)
    e_r = block_e[r // mb]
    off = r - pad_starts[e_r]
    valid = off < counts[e_r]
    a_r = order[jnp.clip(starts[e_r] + off, 0, a_total - 1)]
    row_dst = jnp.where(valid, a_r, 0).astype(jnp.int32)
    row_tok = (row_dst % t).astype(jnp.int32)
    row_w = jnp.where(valid, wts.reshape(a_total)[a_r], 0.0).astype(_F32)
    return block_e, block_cnt, row_tok, row_dst, row_w


def _final_kernel(x_ref, y_ref, lng_ref, lnb_ref, o_ref, *, alpha):
    z = alpha * x_ref[...] + y_ref[0] + y_ref[1]
    o_ref[...] = _layernorm(z, lng_ref[...], lnb_ref[...])


def _final(x1, yslots, lng, lnb, tm, alpha):
    t, d_model = x1.shape
    return pl.pallas_call(
        functools.partial(_final_kernel, alpha=alpha),
        grid=(t // tm,),
        in_specs=[pl.BlockSpec((tm, d_model), lambda i: (i, 0)),
                  pl.BlockSpec((TOP_K, tm, d_model), lambda i: (0, i, 0)),
                  _const_spec(lng.shape), _const_spec(lnb.shape)],
        out_specs=pl.BlockSpec((tm, d_model), lambda i: (i, 0)),
        out_shape=jax.ShapeDtypeStruct((t, d_model), _F32),
        compiler_params=_cparams(1),
        name="moe_residual_ln",
    )(x1, yslots.reshape(TOP_K, t, d_model), lng, lnb)


def _prep_w_in(w):
    d_model = w.shape[0]
    seg = DIFF_HEADS * 2 * DIFF_HEAD_DIM
    o = 0
    cq = w[:, o:o + MLA_Q_LORA]; o += MLA_Q_LORA
    ckv = w[:, o:o + MLA_KV_LORA]; o += MLA_KV_LORA
    kr = w[:, o:o + MLA_ROPE]; o += MLA_ROPE
    rest = w[:, o:]
    z = lambda n: jnp.zeros((d_model, n), w.dtype)
    kr_chunk = jnp.concatenate([z(MLA_NOPE), kr, z(LANE - MLA_NOPE - MLA_ROPE)], axis=1)
    assert MLA_Q_LORA + MLA_KV_LORA + LANE == 4 * LANE and rest.shape[1] == 4 * seg + N_BRANCH * d_model
    return jnp.concatenate([cq, ckv, kr_chunk, rest], axis=1).astype(_BF16)


def _prep_w_uq(w):
    dq = MLA_NOPE + MLA_ROPE
    w3 = w.reshape(w.shape[0], MLA_HEADS, dq)
    w3 = jnp.pad(w3, ((0, 0), (0, 0), (0, LANE - dq)))
    return w3.reshape(w.shape[0], MLA_HEADS * LANE).astype(_BF16)


def _prep_w_ukv(w):
    w3 = w.reshape(w.shape[0], MLA_HEADS, MLA_NOPE + MLA_V)
    wk = jnp.pad(w3[:, :, :MLA_NOPE], ((0, 0), (0, 0), (0, LANE - MLA_NOPE)))
    wv = w3[:, :, MLA_NOPE:]
    return (wk.reshape(w.shape[0], MLA_HEADS * LANE).astype(_BF16),
            wv.reshape(w.shape[0], MLA_HEADS * MLA_V).astype(_BF16))


def kernel(x, positions, w_in, b_gate, mla_q_norm, mla_w_uq, mla_kv_norm, mla_w_ukv, mla_w_out,
           diff_lambda, diff_subln, diff_w_out, ssm_a_re, ssm_a_im, ssm_log_dt, ssm_b_re, ssm_b_im,
           ssm_c_re, ssm_c_im, ssm_d, ssm_w_glu, w_o, ln_gain, ln_bias, router_w, router_bias,
           moe_w_gate, moe_w_up, moe_w_down):
    B, S, D = x.shape
    T = B * S
    depth = w_in.shape[0]
    n_experts = router_w.shape[1]
    epg = n_experts // N_GROUPS
    alpha = float((2 * depth) ** 0.25)
    tm = min(ROW_TILE, T)
    tq = min(ATTN_TILE, S)
    L, P = SSM_CHUNK, SSM_GROUP
    G = ssm_a_re.shape[1]
    assert T % tm == 0 and S % tq == 0 and S % L == 0 and (TOP_K * T) % MOE_ROWS == 0
    assert MLA_HEADS * MLA_V == DIFF_HEADS * 2 * DIFF_HEAD_DIM == G * P

    (inv_a, sel_a), (inv_d, sel_d) = _rope_lane_patterns()
    pos_col = positions.reshape(T, 1).astype(jnp.int32)
    taba = _rope_tables(pos_col, inv_a, sel_a, tm)
    tabd = _rope_tables(pos_col, inv_d, sel_d, tm)

    rw = router_w.T.reshape(N_GROUPS, epg, D).transpose(1, 0, 2).reshape(n_experts, D).astype(_F32)
    rb = router_bias.reshape(N_GROUPS, epg).T.reshape(n_experts, 1).astype(_F32)

    x2d = x.reshape(T, D)
    for l in range(depth):
        lambda_init = 0.8 - 0.6 * math.exp(-0.3 * l)
        w1 = _prep_w_in(w_in[l])
        wuq = _prep_w_uq(mla_w_uq[l])
        wk, wv = _prep_w_ukv(mla_w_ukv[l])
        qm, km, vm, qd, kdv, vd, us, gates = _inproj(
            x2d, taba, tabd, w1, mla_q_norm[l][None].astype(_F32), wuq, mla_kv_norm[l][None].astype(_F32),
            wk, wv, b_gate[l].reshape(1, N_BRANCH * D).astype(_F32), tm)

        o_mla = _mla_attention(qm.reshape(B, S, -1), km.reshape(B, S, -1), vm.reshape(B, S, -1), tq)
        subln2 = jnp.tile(diff_subln[l].astype(_F32), 2)[None]
        o_diff = _diff_attention(qd.reshape(B, S, -1), kdv.reshape(4, B, S, -1), vd.reshape(B, S, -1),
                                 diff_lambda[l].astype(_F32), subln2, tq, lambda_init)

        mats = _ssm_matrices(ssm_a_re[l], ssm_a_im[l], ssm_log_dt[l], ssm_b_re[l], ssm_b_im[l],
                             ssm_c_re[l], ssm_c_im[l], ssm_d[l])
        C = S // L
        u_g = us.reshape(B, C, L, G, P).transpose(3, 1, 0, 2, 4).reshape(G, C * B, L * P)
        y_g = _ssm(u_g, *mats, batch=B)
        ys = y_g.reshape(G, C, B, L, P).transpose(2, 1, 3, 0, 4).reshape(T, G * P)

        x1, idx, wts = _merge(
            o_mla.reshape(T, -1), o_diff.reshape(T, -1), ys, gates, x2d,
            mla_w_out[l].astype(_BF16), diff_w_out[l].astype(_BF16), ssm_w_glu[l].astype(_BF16),
            w_o[l].astype(_BF16), ln_gain[l, 0][None].astype(_F32), ln_bias[l, 0][None].astype(_F32),
            rw, rb, tm, alpha)

        plan = _moe_plan(idx, wts, n_experts, MOE_ROWS)
        yslots = _moe(*plan, x1, moe_w_gate[l].astype(_BF16), moe_w_up[l].astype(_BF16),
                      moe_w_down[l].astype(_BF16))
        x2d = _final(x1, yslots, ln_gain[l, 1][None].astype(_F32), ln_bias[l, 1][None].astype(_F32), tm, alpha)
    return x2d.reshape(B, S, D)
```

```python
import functools
import math

import jax
import jax.numpy as jnp
from jax import lax
from jax.experimental import pallas as pl
from jax.experimental.pallas import tpu as pltpu

MLA_HEADS = 8
MLA_Q_LORA = 256
MLA_KV_LORA = 128
MLA_NOPE = 64
MLA_ROPE = 32
MLA_V = 64
DIFF_HEADS = 8
DIFF_HEAD_DIM = 32
DIFF_ROT = DIFF_HEAD_DIM // 4
SSM_GROUP = 16
SSM_STATE = 64
N_BRANCH = 3
ROPE_THETA = 500000.0
N_GROUPS = 8
TOP_K = 2
LN_EPS = 1e-5
RMS_EPS = 1e-6
SUBLN_EPS = 1e-5

LANE = 128
ROW_TILE = 256
ATTN_TILE = 512
SSM_CHUNK = 16
MOE_ROWS = 256
VMEM_LIMIT = 56 * 1024 * 1024

NEG_BIG = -1e30
LOG2E = math.log2(math.e)

_F32 = jnp.float32
_BF16 = jnp.bfloat16


def _cparams(n_axes):
    return pltpu.CompilerParams(dimension_semantics=("arbitrary",) * n_axes,
                                vmem_limit_bytes=VMEM_LIMIT)


def _dot(a, b):
    return jnp.dot(a, b, preferred_element_type=_F32)


def _dot_nt(a, b, precision=None):
    return lax.dot_general(a, b, (((1,), (1,)), ((), ())), precision=precision,
                           preferred_element_type=_F32)


def _const_spec(shape):
    zeros = (0,) * len(shape)
    return pl.BlockSpec(shape, lambda *_: zeros, pipeline_mode=pl.Buffered(1))


def _rope_table_kernel(pos_ref, inv_ref, sel_ref, out_ref):
    pos = pos_ref[...].astype(_F32)
    ang = pos * inv_ref[...]
    c, s = jnp.cos(ang), jnp.sin(ang)
    sel = sel_ref[...]
    out_ref[0] = c * (sel[0:1] + sel[1:2]) + sel[2:3]
    out_ref[1] = -s * sel[0:1]
    out_ref[2] = s * sel[1:2]


def _rope_tables(pos_col, inv_row, sel_rows, tm):
    t = pos_col.shape[0]
    return pl.pallas_call(
        _rope_table_kernel,
        grid=(t // tm,),
        in_specs=[pl.BlockSpec((tm, 1), lambda i: (i, 0)), _const_spec((1, LANE)), _const_spec((3, LANE))],
        out_specs=pl.BlockSpec((3, tm, LANE), lambda i: (0, i, 0)),
        out_shape=jax.ShapeDtypeStruct((3, t, LANE), _F32),
        compiler_params=_cparams(1),
        name="rope_tables",
    )(pos_col, inv_row, sel_rows)


def _rope_lane_patterns():
    lane = jnp.arange(LANE)
    half_a = MLA_ROPE // 2
    in_rope = (lane >= MLA_NOPE) & (lane < MLA_NOPE + MLA_ROPE)
    fi = (lane - MLA_NOPE) % half_a
    inv_a = jnp.where(in_rope, ROPE_THETA ** (-(2.0 * fi.astype(_F32)) / MLA_ROPE), 0.0)
    x1_a = in_rope & (lane < MLA_NOPE + half_a)
    x2_a = in_rope & ~x1_a
    sel_a = jnp.stack([x1_a, x2_a, ~in_rope]).astype(_F32)
    half_d = DIFF_ROT // 2
    d = lane % DIFF_HEAD_DIM
    in_rot = d < DIFF_ROT
    inv_d = jnp.where(in_rot, ROPE_THETA ** (-(2.0 * (d % half_d).astype(_F32)) / DIFF_ROT), 0.0)
    x1_d = d < half_d
    x2_d = in_rot & ~x1_d
    sel_d = jnp.stack([x1_d, x2_d, ~in_rot]).astype(_F32)
    return (inv_a[None].astype(_F32), sel_a), (inv_d[None].astype(_F32), sel_d)


def _rope_chunk(x, tab_ref, half):
    return (x * tab_ref[0] + pltpu.roll(x, LANE - half, 1) * tab_ref[1]
            + pltpu.roll(x, half, 1) * tab_ref[2])


def _rms(x, g, eps):
    return x * lax.rsqrt(jnp.mean(x * x, axis=-1, keepdims=True) + eps) * g


def _inproj_kernel(x_ref, taba_ref, tabd_ref, w1_ref, qn_ref, wuq_ref, kvn_ref, wk_ref, wv_ref, bg_ref,
                   qmt_ref, km_ref, vmt_ref, qdt_ref, kd_ref, vdt_ref, us_ref, gates_ref,
                   *, q_scale_mla, q_scale_diff, d_model):
    xb = x_ref[...].astype(_BF16)
    n_mla = MLA_HEADS
    lat = _dot(xb, w1_ref[:, 0:4 * LANE])
    cqn = _rms(lat[:, :MLA_Q_LORA], qn_ref[...], RMS_EPS).astype(_BF16)
    q = _dot(cqn, wuq_ref[...])
    for h in range(n_mla):
        sl = slice(h * LANE, (h + 1) * LANE)
        qh = _rope_chunk(q[:, sl], taba_ref, MLA_ROPE // 2) * q_scale_mla
        qmt_ref[0, sl, :] = qh.T.astype(_BF16)
    ckvn = _rms(lat[:, MLA_Q_LORA:MLA_Q_LORA + MLA_KV_LORA], kvn_ref[...], RMS_EPS).astype(_BF16)
    kn = _dot(ckvn, wk_ref[...])
    kr = _rope_chunk(lat[:, 3 * LANE:4 * LANE], taba_ref, MLA_ROPE // 2)
    for h in range(n_mla):
        sl = slice(h * LANE, (h + 1) * LANE)
        km_ref[:, sl] = (kn[:, sl] + kr).astype(_BF16)
    vm = _dot(ckvn, wv_ref[...])
    seg = DIFF_HEADS * 2 * DIFF_HEAD_DIM
    for c in range(seg // LANE):
        sl = slice(c * LANE, (c + 1) * LANE)
        vmt_ref[0, sl, :] = vm[:, sl].T.astype(_BF16)

    base = 4 * LANE
    lane = lax.broadcasted_iota(jnp.int32, (1, LANE), 1)
    comp = lane // DIFF_HEAD_DIM
    qd = _dot(xb, w1_ref[:, base:base + seg])
    kd = _dot(xb, w1_ref[:, base + seg:base + 2 * seg])
    vd = _dot(xb, w1_ref[:, base + 2 * seg:base + 3 * seg])
    for c in range(seg // LANE):
        sl = slice(c * LANE, (c + 1) * LANE)
        qc = _rope_chunk(qd[:, sl], tabd_ref, DIFF_ROT // 2) * q_scale_diff
        qdt_ref[0, sl, :] = qc.T.astype(_BF16)
        kc = _rope_chunk(kd[:, sl], tabd_ref, DIFF_ROT // 2)
        for v in range(4):
            kd_ref[v, :, sl] = jnp.where(comp == v, kc, 0.0).astype(_BF16)
        vdt_ref[0, sl, :] = vd[:, sl].T.astype(_BF16)
    us_ref[...] = _dot(xb, w1_ref[:, base + 3 * seg:base + 4 * seg]).astype(_BF16)
    gbase = base + 4 * seg
    for c in range(N_BRANCH):
        sl = slice(c * d_model, (c + 1) * d_model)
        g = _dot(xb, w1_ref[:, gbase + c * d_model:gbase + (c + 1) * d_model]) + bg_ref[:, sl]
        gates_ref[:, sl] = jax.nn.sigmoid(g).astype(_BF16)


def _inproj(x2d, taba, tabd, w1, qn, wuq, kvn, wk, wv, bg, tm):
    t, d_model = x2d.shape
    nt = t // tm
    seg = DIFF_HEADS * 2 * DIFF_HEAD_DIM
    hm = MLA_HEADS * LANE
    row = lambda w: pl.BlockSpec((tm, w), lambda i: (i, 0))
    colt = lambda w: pl.BlockSpec((1, w, tm), lambda i: (i, 0, 0))
    tab = pl.BlockSpec((3, tm, LANE), lambda i: (0, i, 0))
    kern = functools.partial(
        _inproj_kernel,
        q_scale_mla=float((MLA_NOPE + MLA_ROPE) ** -0.5 * LOG2E),
        q_scale_diff=float(DIFF_HEAD_DIM ** -0.5 * LOG2E),
        d_model=d_model)
    out_shapes = (
        jax.ShapeDtypeStruct((nt, hm, tm), _BF16), jax.ShapeDtypeStruct((t, hm), _BF16),
        jax.ShapeDtypeStruct((nt, MLA_HEADS * MLA_V, tm), _BF16),
        jax.ShapeDtypeStruct((nt, seg, tm), _BF16), jax.ShapeDtypeStruct((4, t, seg), _BF16),
        jax.ShapeDtypeStruct((nt, seg, tm), _BF16), jax.ShapeDtypeStruct((t, seg), _BF16),
        jax.ShapeDtypeStruct((t, N_BRANCH * d_model), _BF16))
    out_specs = (colt(hm), row(hm), colt(MLA_HEADS * MLA_V), colt(seg),
                 pl.BlockSpec((4, tm, seg), lambda i: (0, i, 0)), colt(seg), row(seg), row(N_BRANCH * d_model))
    return pl.pallas_call(
        kern,
        grid=(nt,),
        in_specs=[row(d_model), tab, tab, _const_spec(w1.shape), _const_spec(qn.shape), _const_spec(wuq.shape),
                  _const_spec(kvn.shape), _const_spec(wk.shape), _const_spec(wv.shape), _const_spec(bg.shape)],
        out_specs=out_specs,
        out_shape=out_shapes,
        compiler_params=_cparams(1),
        name="inproj",
    )(x2d, taba, tabd, w1, qn, wuq, kvn, wk, wv, bg)


def _flash_maps(q_ts, load_ks, load_vt, qi, tq, s_scrs):
    n = len(q_ts)

    def qk(i, j):
        s_scrs[i][...] = _dot(load_ks[i](j), q_ts[i])

    def step(j, carry, diagonal):
        vt = load_vt(j)
        if diagonal:
            r = lax.broadcasted_iota(jnp.int32, (tq, tq), 0)
            c = lax.broadcasted_iota(jnp.int32, (tq, tq), 1)
            keep = r <= c
        out = []
        for i in range(n):
            if i + 1 < n:
                qk(i + 1, j)
            elif not diagonal:
                qk(0, j + 1)
            m, l, acc = carry[i]
            s = s_scrs[i][...]
            if diagonal:
                s = jnp.where(keep, s, NEG_BIG)
            m_new = jnp.maximum(m, jnp.max(s, axis=0, keepdims=True))
            alpha = jnp.exp2(m - m_new)
            p = jnp.exp2(s - m_new)
            l = alpha * l + jnp.sum(p, axis=0, keepdims=True)
            acc = alpha * acc + _dot(vt, p.astype(_BF16))
            out.append((m_new, l, acc))
        return tuple(out)

    init = tuple((jnp.full((1, tq), NEG_BIG, _F32), jnp.zeros((1, tq), _F32), jnp.zeros((LANE, tq), _F32))
                 for _ in range(n))
    qk(0, 0)
    carry = lax.fori_loop(0, qi, lambda j, c: step(j, c, False), init)
    carry = step(qi, carry, True)
    return [acc / l for (_, l, acc) in carry]


def _key_block(k_ref, lead, j, tq, lanes):
    return k_ref[lead + (pl.ds(pl.multiple_of(j * tq, tq), tq), lanes)]


def _mla_attn_kernel(qt_ref, k_ref, vt_ref, o_ref, *s_scrs, tq):
    qi = pl.program_id(2)
    sls = [slice(hh * LANE, (hh + 1) * LANE) for hh in range(2)]
    outs = _flash_maps([qt_ref[0, sl, :] for sl in sls],
                       [lambda j, sl=sl: _key_block(k_ref, (0,), j, tq, sl) for sl in sls],
                       lambda j: vt_ref[0, j], qi, tq, s_scrs)
    ot = jnp.concatenate([outs[0][:MLA_V], outs[1][MLA_V:]], axis=0)
    o_ref[0] = ot.T.astype(o_ref.dtype)


def _mla_attention(qt, k, vt, b, tq):
    s = k.shape[1]
    nq = s // tq
    pairs = MLA_HEADS // 2
    vt4 = vt.reshape(b, nq, vt.shape[1], tq)
    return pl.pallas_call(
        functools.partial(_mla_attn_kernel, tq=tq),
        grid=(b, pairs, nq),
        in_specs=[pl.BlockSpec((1, 2 * LANE, tq), lambda bi, p, qi: (bi * nq + qi, p, 0)),
                  pl.BlockSpec((1, s, 2 * LANE), lambda bi, p, qi: (bi, 0, p)),
                  pl.BlockSpec((1, nq, LANE, tq), lambda bi, p, qi: (bi, 0, p, 0))],
        out_specs=pl.BlockSpec((1, tq, LANE), lambda bi, p, qi: (bi, qi, p)),
        out_shape=jax.ShapeDtypeStruct((b, s, pairs * LANE), _BF16),
        scratch_shapes=[pltpu.VMEM((tq, tq), _F32)] * 2,
        compiler_params=_cparams(3),
        name="mla_attention",
    )(qt, k, vt4)


def _diff_attn_kernel(qt_ref, k_ref, vt_ref, lam_ref, subln_ref, o_ref, *s_scrs, tq, lambda_init):
    qi = pl.program_id(2)
    qt = qt_ref[0]
    outs = _flash_maps([qt] * 4,
                       [lambda j, v=v: _key_block(k_ref, (v, 0), j, tq, slice(None)) for v in range(4)],
                       lambda j: vt_ref[0, j], qi, tq, s_scrs)
    lf = lam_ref[...]
    lam = (jnp.exp(jnp.sum(lf[0:1] * lf[1:2], axis=-1, keepdims=True))
           - jnp.exp(jnp.sum(lf[2:3] * lf[3:4], axis=-1, keepdims=True)) + lambda_init)
    half = 2 * DIFF_HEAD_DIM
    dt = jnp.concatenate([(outs[0] - lam * outs[1])[:half], (outs[2] - lam * outs[3])[half:]], axis=0)
    d = dt.T
    lane = lax.broadcasted_iota(jnp.int32, (tq, LANE), 1)
    first = lane < half
    sq = d * d
    ss_a = jnp.sum(jnp.where(first, sq, 0.0), axis=-1, keepdims=True)
    ss_b = jnp.sum(jnp.where(first, 0.0, sq), axis=-1, keepdims=True)
    ms = jnp.where(first, ss_a, ss_b) * (1.0 / half)
    o_ref[0] = (d * lax.rsqrt(ms + SUBLN_EPS) * subln_ref[...] * (1.0 - lambda_init)).astype(o_ref.dtype)


def _diff_attention(qt, kvar, vt, lam, subln2, b, tq, lambda_init):
    s = kvar.shape[2]
    nq = s // tq
    pairs = DIFF_HEADS // 2
    vt4 = vt.reshape(b, nq, vt.shape[1], tq)
    return pl.pallas_call(
        functools.partial(_diff_attn_kernel, tq=tq, lambda_init=lambda_init),
        grid=(b, pairs, nq),
        in_specs=[pl.BlockSpec((1, LANE, tq), lambda bi, p, qi: (bi * nq + qi, p, 0)),
                  pl.BlockSpec((4, 1, s, LANE), lambda bi, p, qi: (0, bi, 0, p)),
                  pl.BlockSpec((1, nq, LANE, tq), lambda bi, p, qi: (bi, 0, p, 0)),
                  _const_spec(lam.shape), _const_spec(subln2.shape)],
        out_specs=pl.BlockSpec((1, tq, LANE), lambda bi, p, qi: (bi, qi, p)),
        out_shape=jax.ShapeDtypeStruct((b, s, pairs * LANE), _BF16),
        scratch_shapes=[pltpu.VMEM((tq, tq), _F32)] * 4,
        compiler_params=_cparams(3),
        name="diff_attention",
    )(qt, kvar, vt4, lam, subln2)


def _ssm_kernel(u_ref, t_ref, b_ref, c_ref, ar_ref, ai_ref, d_ref, y_ref, x_scr, h_scr, *, batch, chunks):
    u = u_ref[0]
    x_scr[...] = _dot(u, b_ref[0])
    ar, ai = ar_ref[0], ai_ref[0]

    def step(c, h):
        r0 = c * batch
        if batch % 8 == 0:
            r0 = pl.multiple_of(r0, 8)
        h_scr[pl.ds(r0, batch), :] = h
        return ar * h + ai * pltpu.roll(h, SSM_STATE, 1) + x_scr[pl.ds(r0, batch), :]

    lax.fori_loop(0, chunks, step, jnp.zeros((batch, 2 * SSM_STATE), _F32))
    y = _dot(u, t_ref[0]) + _dot(h_scr[...].astype(_BF16), c_ref[0]) + u.astype(_F32) * d_ref[0]
    y_ref[0] = jax.nn.gelu(y).astype(y_ref.dtype)


def _ssm(u_g, tmat, bmat, cmat, ar, ai, dflat, batch):
    g, r, w = u_g.shape
    per_g = lambda shape: pl.BlockSpec((1,) + shape, lambda i: (i, 0, 0))
    return pl.pallas_call(
        functools.partial(_ssm_kernel, batch=batch, chunks=r // batch),
        grid=(g,),
        in_specs=[per_g((r, w)), per_g((w, w)), per_g((w, 2 * SSM_STATE)), per_g((2 * SSM_STATE, w)),
                  per_g((1, 2 * SSM_STATE)), per_g((1, 2 * SSM_STATE)), per_g((1, w))],
        out_specs=per_g((r, w)),
        out_shape=jax.ShapeDtypeStruct((g, r, w), _BF16),
        scratch_shapes=[pltpu.VMEM((r, 2 * SSM_STATE), _F32), pltpu.VMEM((r, 2 * SSM_STATE), _F32)],
        compiler_params=_cparams(1),
        name="ssm_scan",
    )(u_g, tmat, bmat, cmat, ar, ai, dflat)


def _ssm_matrices(a_re, a_im, log_dt, b_re, b_im, c_re, c_im, d):
    L, P, N = SSM_CHUNK, SSM_GROUP, SSM_STATE
    G = a_re.shape[0]
    A = lax.complex(a_re.astype(_F32), a_im.astype(_F32))
    dt = jnp.exp(log_dt.astype(_F32))[:, None]
    a_bar = jnp.exp(A * dt)
    b_bar = ((a_bar - 1.0) / A)[..., None] * lax.complex(b_re.astype(_F32), b_im.astype(_F32))
    C = lax.complex(c_re.astype(_F32), c_im.astype(_F32))
    steps = jnp.arange(L + 1, dtype=_F32)
    apow = jnp.exp((A * dt)[None] * steps[:, None, None])
    m = jnp.real(jnp.einsum('gpn,dgn,gnq->dgpq', C, apow[:L], b_bar))
    s_idx = jnp.arange(L)[:, None]
    t_idx = jnp.arange(L)[None, :]
    lag = jnp.clip(t_idx - s_idx, 0, L - 1)
    tfull = jnp.where((t_idx >= s_idx)[:, :, None, None, None], m[lag], 0.0)
    tmat = tfull.transpose(2, 0, 4, 1, 3).reshape(G, L * P, L * P)
    bm = apow[L - 1 - jnp.arange(L)][:, :, :, None] * b_bar[None]
    bm = bm.transpose(1, 0, 3, 2).reshape(G, L * P, N)
    bmat = jnp.concatenate([jnp.real(bm), jnp.imag(bm)], axis=-1)
    cm = C[None] * apow[1:L + 1][:, :, None, :]
    cm = cm.transpose(1, 3, 0, 2).reshape(G, N, L * P)
    cmat = jnp.concatenate([jnp.real(cm), -jnp.imag(cm)], axis=1)
    al = apow[L]
    ar = jnp.concatenate([jnp.real(al), jnp.real(al)], axis=-1)[:, None, :]
    ai = jnp.concatenate([-jnp.imag(al), jnp.imag(al)], axis=-1)[:, None, :]
    dflat = jnp.tile(d.astype(_F32).reshape(G, 1, P), (1, L, 1)).reshape(G, 1, L * P)
    return tmat.astype(_BF16), bmat.astype(_BF16), cmat.astype(_BF16), ar, ai, dflat


def _layernorm(z, g, b):
    mu = jnp.mean(z, axis=-1, keepdims=True)
    zc = z - mu
    var = jnp.mean(zc * zc, axis=-1, keepdims=True)
    return zc * lax.rsqrt(var + LN_EPS) * g + b


def _merge_kernel(om_ref, od_ref, ys_ref, g_ref, x_ref, wmo_ref, wdo_ref, wglu_ref, wo_ref, lng_ref, lnb_ref,
                  rw_ref, rb_ref, x1_ref, idx_ref, wts_ref, cnt_ref, *, alpha, d_model):
    ya = _dot(om_ref[...], wmo_ref[...])
    yb = _dot(od_ref[...], wdo_ref[...])
    gl = _dot(ys_ref[...], wglu_ref[...])
    yc = gl[:, :d_model] * jax.nn.sigmoid(gl[:, d_model:])
    merged = (g_ref[:, 0:d_model].astype(_F32) * ya + g_ref[:, d_model:2 * d_model].astype(_F32) * yb
              + g_ref[:, 2 * d_model:3 * d_model].astype(_F32) * yc)
    z = alpha * x_ref[...] + _dot(merged.astype(_BF16), wo_ref[...])
    x1 = _layernorm(z, lng_ref[...], lnb_ref[...])
    x1_ref[...] = x1

    logits = _dot_nt(rw_ref[...], x1, precision=lax.Precision.HIGHEST)
    scores = jax.nn.sigmoid(logits)
    biased = scores + rb_ref[...]
    epg = logits.shape[0] // N_GROUPS
    a = [biased[j * N_GROUPS:(j + 1) * N_GROUPS] for j in range(epg)]
    sc = [scores[j * N_GROUPS:(j + 1) * N_GROUPS] for j in range(epg)]
    gs = None
    for i in range(epg):
        for j in range(i + 1, epg):
            pair = a[i] + a[j]
            gs = pair if gs is None else jnp.maximum(gs, pair)
    giota = lax.broadcasted_iota(jnp.int32, gs.shape, 0)
    gmax = jnp.max(gs, axis=0, keepdims=True)
    gsel = jnp.min(jnp.where(gs == gmax, giota, N_GROUPS), axis=0, keepdims=True)
    hot = giota == gsel
    val = [jnp.sum(jnp.where(hot, a[j], 0.0), axis=0, keepdims=True) for j in range(epg)]
    raw = [jnp.sum(jnp.where(hot, sc[j], 0.0), axis=0, keepdims=True) for j in range(epg)]
    b1, i1, w1 = val[0], jnp.zeros_like(gsel), raw[0]
    for j in range(1, epg):
        take = val[j] > b1
        b1 = jnp.where(take, val[j], b1)
        i1 = jnp.where(take, j, i1)
        w1 = jnp.where(take, raw[j], w1)
    b2 = jnp.full_like(b1, -jnp.inf)
    i2 = jnp.zeros_like(gsel)
    w2 = jnp.zeros_like(w1)
    for j in range(epg):
        take = (i1 != j) & (val[j] > b2)
        b2 = jnp.where(take, val[j], b2)
        i2 = jnp.where(take, j, i2)
        w2 = jnp.where(take, raw[j], w2)
    wsum = w1 + w2
    e1, e2 = gsel * epg + i1, gsel * epg + i2
    idx_ref[...] = jnp.concatenate([e1, e2], axis=0)
    wts_ref[...] = jnp.concatenate([w1 / wsum, w2 / wsum], axis=0)

    eiota = lax.broadcasted_iota(jnp.int32, logits.shape, 0)
    hit = (eiota == e1).astype(jnp.int32) + (eiota == e2).astype(jnp.int32)
    part = hit[:, 0:LANE]
    for c in range(1, hit.shape[1] // LANE):
        part = part + hit[:, c * LANE:(c + 1) * LANE]

    @pl.when(pl.program_id(0) == 0)
    def _():
        cnt_ref[...] = jnp.zeros_like(cnt_ref)

    cnt_ref[...] += part


def _merge(om, od, ys, gates, x2d, wmo, wdo, wglu, wo, lng, lnb, rw, rb, tm, alpha):
    t, d_model = x2d.shape
    n_experts = rw.shape[0]
    row = lambda w: pl.BlockSpec((tm, w), lambda i: (i, 0))
    col = pl.BlockSpec((TOP_K, tm), lambda i: (0, i))
    return pl.pallas_call(
        functools.partial(_merge_kernel, alpha=alpha, d_model=d_model),
        grid=(t // tm,),
        in_specs=[row(om.shape[1]), row(od.shape[1]), row(ys.shape[1]), row(gates.shape[1]), row(d_model),
                  _const_spec(wmo.shape), _const_spec(wdo.shape), _const_spec(wglu.shape), _const_spec(wo.shape),
                  _const_spec(lng.shape), _const_spec(lnb.shape), _const_spec(rw.shape), _const_spec(rb.shape)],
        out_specs=(row(d_model), col, col, pl.BlockSpec((n_experts, LANE), lambda i: (0, 0))),
        out_shape=(jax.ShapeDtypeStruct((t, d_model), _F32), jax.ShapeDtypeStruct((TOP_K, t), jnp.int32),
                   jax.ShapeDtypeStruct((TOP_K, t), _F32), jax.ShapeDtypeStruct((n_experts, LANE), jnp.int32)),
        compiler_params=_cparams(1),
        name="merge_router",
    )(om, od, ys, gates, x2d, wmo, wdo, wglu, wo, lng, lnb, rw, rb)


DMA_UNROLL = 8


def _moe_kernel(be_ref, cnt_ref, nu_ref, tok_ref, tokn_ref, dst_ref, x_hbm, wg_ref, wu_ref, wd_ref, out_hbm,
                xbuf, ybuf, wgb, wub, wdb, sem_in, sem_out, *, mb):
    i = pl.program_id(0)
    n_used = nu_ref[0]
    slot = lax.rem(i, 2)

    def for_rows(start_row):
        def body(j, carry):
            for u in range(DMA_UNROLL):
                start_row(j * DMA_UNROLL + u, u % 2)
            return carry
        lax.fori_loop(0, mb // DMA_UNROLL, body, 0)

    def for_first_rows(n, fn):
        def body(r, carry):
            fn(r)
            return carry
        lax.fori_loop(0, n, body, 0)

    def gather_row(t_ref, s, r):
        return pltpu.make_async_copy(x_hbm.at[pl.ds(t_ref[0, 0, r], 1), :], xbuf.at[s, pl.ds(r, 1), :], sem_in.at[s])

    def scatter_row(s, r, dst):
        return pltpu.make_async_copy(ybuf.at[s, pl.ds(r, 1), :], out_hbm.at[pl.ds(dst, 1), :], sem_out.at[s])

    def issue_gather(t_ref, s):
        for_rows(lambda r, pr: gather_row(t_ref, s, r).start(priority=pr))

    def wait_gather(s):
        pltpu.make_async_copy(x_hbm.at[pl.ds(0, mb), :], xbuf.at[s], sem_in.at[s]).wait()

    def issue_scatter(s, cnt):
        @pl.when(cnt == mb)
        def _():
            for_rows(lambda r, pr: scatter_row(s, r, dst_ref[0, 0, r]).start(priority=pr))

        @pl.when(cnt < mb)
        def _():
            for_first_rows(cnt, lambda r: scatter_row(s, r, dst_ref[0, 0, r]).start())

    def wait_scatter(s, cnt):
        @pl.when(cnt == mb)
        def _():
            pltpu.make_async_copy(ybuf.at[s], out_hbm.at[pl.ds(0, mb), :], sem_out.at[s]).wait()

        @pl.when(cnt < mb)
        def _():
            for_first_rows(cnt, lambda r: scatter_row(s, r, 0).wait())

    @pl.when(i < n_used)
    def _():
        @pl.when(i == 0)
        def _():
            issue_gather(tok_ref, 0)

        @pl.when(i + 1 < n_used)
        def _():
            issue_gather(tokn_ref, 1 - slot)

        @pl.when((i == 0) | (be_ref[i] != be_ref[jnp.maximum(i - 1, 0)]))
        def _():
            wgb[...] = wg_ref[0].astype(_BF16)
            wub[...] = wu_ref[0].astype(_BF16)
            wdb[...] = wd_ref[0].astype(_BF16)

        wait_gather(slot)

        @pl.when(i >= 2)
        def _():
            wait_scatter(slot, cnt_ref[jnp.maximum(i - 2, 0)])

        xb = xbuf[slot].astype(_BF16)
        h = jax.nn.silu(_dot(xb, wgb[...])) * _dot(xb, wub[...])
        ybuf[slot] = _dot(h.astype(_BF16), wdb[...])
        issue_scatter(slot, cnt_ref[i])

        @pl.when(i == n_used - 1)
        def _():
            @pl.when(i >= 1)
            def _():
                wait_scatter(1 - slot, cnt_ref[jnp.maximum(i - 1, 0)])
            wait_scatter(slot, cnt_ref[i])


def _moe(block_e, block_cnt, n_used, row_tok, row_dst, x1, wg, wu, wd):
    t, d_model = x1.shape
    n_blocks, mb = row_tok.shape
    ff = wg.shape[-1]
    assert mb % DMA_UNROLL == 0
    smem_row = lambda f: pl.BlockSpec((1, 1, mb), lambda i, be, cnt, nu: (f(i), 0, 0), memory_space=pltpu.SMEM)
    wspec = lambda a, b: pl.BlockSpec((1, a, b), lambda i, be, cnt, nu: (be[i], 0, 0))
    grid_spec = pltpu.PrefetchScalarGridSpec(
        num_scalar_prefetch=3,
        grid=(n_blocks,),
        in_specs=[smem_row(lambda i: i), smem_row(lambda i: jnp.minimum(i + 1, n_blocks - 1)), smem_row(lambda i: i),
                  pl.BlockSpec(memory_space=pl.ANY),
                  wspec(d_model, ff), wspec(d_model, ff), wspec(ff, d_model)],
        out_specs=pl.BlockSpec(memory_space=pl.ANY),
        scratch_shapes=[pltpu.VMEM((2, mb, d_model), _F32), pltpu.VMEM((2, mb, d_model), _F32),
                        pltpu.VMEM((d_model, ff), _BF16), pltpu.VMEM((d_model, ff), _BF16),
                        pltpu.VMEM((ff, d_model), _BF16),
                        pltpu.SemaphoreType.DMA((2,)), pltpu.SemaphoreType.DMA((2,))])
    tok3 = row_tok.reshape(n_blocks, 1, mb)
    return pl.pallas_call(
        functools.partial(_moe_kernel, mb=mb),
        grid_spec=grid_spec,
        out_shape=jax.ShapeDtypeStruct((TOP_K * t, d_model), _F32),
        compiler_params=_cparams(1),
        name="moe_experts",
    )(block_e, block_cnt, n_used, tok3, tok3, row_dst.reshape(n_blocks, 1, mb), x1, wg, wu, wd)


def _moe_plan(idx, counts, mb):
    k, t = idx.shape
    a_total = k * t
    n_experts = counts.shape[0]
    e_flat = idx.reshape(a_total)
    order = jnp.argsort(e_flat).astype(jnp.int32)
    starts = jnp.cumsum(counts) - counts
    padded = (counts + mb - 1) // mb * mb
    pad_ends = jnp.cumsum(padded)
    pad_starts = pad_ends - padded
    n_blocks = -(-a_total // mb) + n_experts
    blk = jnp.arange(n_blocks, dtype=jnp.int32)
    blk_start = blk * mb
    block_e = jnp.minimum(jnp.sum((pad_ends[None, :] <= blk_start[:, None]).astype(jnp.int32), axis=1),
                          n_experts - 1)
    off0 = blk_start - pad_starts[block_e]
    cnt = jnp.clip(counts[block_e] - off0, 0, mb)
    n_used = jnp.sum((cnt > 0).astype(jnp.int32)).reshape(1)
    lane = jnp.arange(mb, dtype=jnp.int32)[None, :]
    valid = lane < cnt[:, None]
    src = jnp.clip((starts[block_e] + off0)[:, None] + lane, 0, a_total - 1)
    a_r = order[src]
    row_dst = jnp.where(valid, a_r, 0).astype(jnp.int32)
    row_tok = jnp.where(valid, a_r % t, 0).astype(jnp.int32)
    return block_e.astype(jnp.int32), cnt.astype(jnp.int32), n_used, row_tok, row_dst


def _final_kernel(x_ref, y0_ref, y1_ref, w_ref, lng_ref, lnb_ref, o_ref, *, alpha):
    w = w_ref[...]
    z = alpha * x_ref[...] + (y0_ref[...] * w[:, 0:1] + y1_ref[...] * w[:, 1:2])
    o_ref[...] = _layernorm(z, lng_ref[...], lnb_ref[...])


def _final(x1, yslots, wts_t, lng, lnb, tm, alpha):
    t, d_model = x1.shape
    nt = t // tm
    return pl.pallas_call(
        functools.partial(_final_kernel, alpha=alpha),
        grid=(nt,),
        in_specs=[pl.BlockSpec((tm, d_model), lambda i: (i, 0)),
                  pl.BlockSpec((tm, d_model), lambda i: (i, 0)),
                  pl.BlockSpec((tm, d_model), lambda i: (nt + i, 0)),
                  pl.BlockSpec((tm, TOP_K), lambda i: (i, 0)),
                  _const_spec(lng.shape), _const_spec(lnb.shape)],
        out_specs=pl.BlockSpec((tm, d_model), lambda i: (i, 0)),
        out_shape=jax.ShapeDtypeStruct((t, d_model), _F32),
        compiler_params=_cparams(1),
        name="moe_residual_ln",
    )(x1, yslots, yslots, wts_t, lng, lnb)


def _prep_w_in(w):
    d_model = w.shape[0]
    seg = DIFF_HEADS * 2 * DIFF_HEAD_DIM
    o = 0
    cq = w[:, o:o + MLA_Q_LORA]; o += MLA_Q_LORA
    ckv = w[:, o:o + MLA_KV_LORA]; o += MLA_KV_LORA
    kr = w[:, o:o + MLA_ROPE]; o += MLA_ROPE
    rest = w[:, o:]
    z = lambda n: jnp.zeros((d_model, n), w.dtype)
    kr_chunk = jnp.concatenate([z(MLA_NOPE), kr, z(LANE - MLA_NOPE - MLA_ROPE)], axis=1)
    assert MLA_Q_LORA + MLA_KV_LORA + LANE == 4 * LANE and rest.shape[1] == 4 * seg + N_BRANCH * d_model
    return jnp.concatenate([cq, ckv, kr_chunk, rest], axis=1).astype(_BF16)


def _prep_w_uq(w):
    dq = MLA_NOPE + MLA_ROPE
    w3 = w.reshape(w.shape[0], MLA_HEADS, dq)
    w3 = jnp.pad(w3, ((0, 0), (0, 0), (0, LANE - dq)))
    return w3.reshape(w.shape[0], MLA_HEADS * LANE).astype(_BF16)


def _prep_w_ukv(w):
    w3 = w.reshape(w.shape[0], MLA_HEADS, MLA_NOPE + MLA_V)
    wk = jnp.pad(w3[:, :, :MLA_NOPE], ((0, 0), (0, 0), (0, LANE - MLA_NOPE)))
    wv = w3[:, :, MLA_NOPE:]
    return (wk.reshape(w.shape[0], MLA_HEADS * LANE).astype(_BF16),
            wv.reshape(w.shape[0], MLA_HEADS * MLA_V).astype(_BF16))


def kernel(x, positions, w_in, b_gate, mla_q_norm, mla_w_uq, mla_kv_norm, mla_w_ukv, mla_w_out,
           diff_lambda, diff_subln, diff_w_out, ssm_a_re, ssm_a_im, ssm_log_dt, ssm_b_re, ssm_b_im,
           ssm_c_re, ssm_c_im, ssm_d, ssm_w_glu, w_o, ln_gain, ln_bias, router_w, router_bias,
           moe_w_gate, moe_w_up, moe_w_down):
    B, S, D = x.shape
    T = B * S
    depth = w_in.shape[0]
    n_experts = router_w.shape[1]
    epg = n_experts // N_GROUPS
    alpha = float((2 * depth) ** 0.25)
    tm = min(ROW_TILE, T)
    tq = min(ATTN_TILE, S)
    L, P = SSM_CHUNK, SSM_GROUP
    G = ssm_a_re.shape[1]
    assert T % tm == 0 and S % tq == 0 and S % L == 0 and (TOP_K * T) % MOE_ROWS == 0
    assert MLA_HEADS * MLA_V == DIFF_HEADS * 2 * DIFF_HEAD_DIM == G * P

    (inv_a, sel_a), (inv_d, sel_d) = _rope_lane_patterns()
    pos_col = positions.reshape(T, 1).astype(jnp.int32)
    taba = _rope_tables(pos_col, inv_a, sel_a, tm)
    tabd = _rope_tables(pos_col, inv_d, sel_d, tm)

    rw = router_w.T.reshape(N_GROUPS, epg, D).transpose(1, 0, 2).reshape(n_experts, D).astype(_F32)
    rb = router_bias.reshape(N_GROUPS, epg).T.reshape(n_experts, 1).astype(_F32)

    x2d = x.reshape(T, D)
    for l in range(depth):
        lambda_init = 0.8 - 0.6 * math.exp(-0.3 * l)
        w1 = _prep_w_in(w_in[l])
        wuq = _prep_w_uq(mla_w_uq[l])
        wk, wv = _prep_w_ukv(mla_w_ukv[l])
        qmt, km, vmt, qdt, kdv, vdt, us, gates = _inproj(
            x2d, taba, tabd, w1, mla_q_norm[l][None].astype(_F32), wuq, mla_kv_norm[l][None].astype(_F32),
            wk, wv, b_gate[l].reshape(1, N_BRANCH * D).astype(_F32), tq)

        o_mla = _mla_attention(qmt, km.reshape(B, S, -1), vmt, B, tq)
        subln2 = jnp.tile(diff_subln[l].astype(_F32), 2)[None]
        o_diff = _diff_attention(qdt, kdv.reshape(4, B, S, -1), vdt, diff_lambda[l].astype(_F32), subln2,
                                 B, tq, lambda_init)

        mats = _ssm_matrices(ssm_a_re[l], ssm_a_im[l], ssm_log_dt[l], ssm_b_re[l], ssm_b_im[l],
                             ssm_c_re[l], ssm_c_im[l], ssm_d[l])
        C = S // L
        u_g = us.reshape(B, C, L, G, P).transpose(3, 1, 0, 2, 4).reshape(G, C * B, L * P)
        y_g = _ssm(u_g, *mats, batch=B)
        ys = y_g.reshape(G, C, B, L, P).transpose(2, 1, 3, 0, 4).reshape(T, G * P)

        x1, idx, wts, cnt_part = _merge(
            o_mla.reshape(T, -1), o_diff.reshape(T, -1), ys, gates, x2d,
            mla_w_out[l].astype(_BF16), diff_w_out[l].astype(_BF16), ssm_w_glu[l].astype(_BF16),
            w_o[l].astype(_BF16), ln_gain[l, 0][None].astype(_F32), ln_bias[l, 0][None].astype(_F32),
            rw, rb, tm, alpha)

        plan = _moe_plan(idx, jnp.sum(cnt_part, axis=1), MOE_ROWS)
        yslots = _moe(*plan, x1, moe_w_gate[l].astype(_F32), moe_w_up[l].astype(_F32), moe_w_down[l].astype(_F32))
        x2d = _final(x1, yslots, wts.T, ln_gain[l, 1][None].astype(_F32), ln_bias[l, 1][None].astype(_F32),
                     tm, alpha)
    return x2d.reshape(B, S, D)
```

```python
import functools
import math

import jax
import jax.numpy as jnp
from jax import lax
from jax.experimental import pallas as pl
from jax.experimental.pallas import tpu as pltpu

MLA_HEADS = 8
MLA_Q_LORA = 256
MLA_KV_LORA = 128
MLA_NOPE = 64
MLA_ROPE = 32
MLA_V = 64
DIFF_HEADS = 8
DIFF_HEAD_DIM = 32
DIFF_ROT = DIFF_HEAD_DIM // 4
SSM_GROUP = 16
SSM_STATE = 64
N_BRANCH = 3
ROPE_THETA = 500000.0
N_GROUPS = 8
TOP_K = 2
LN_EPS = 1e-5
RMS_EPS = 1e-6
SUBLN_EPS = 1e-5

LANE = 128
ROW_TILE = 256
ATTN_TILE = 512
SSM_CHUNK = 16
MOE_ROWS = 256
MERGE_TILE = 512
MERGE_SUBTILES = 2
TOKEN_TILE = 8
VMEM_LIMIT = 56 * 1024 * 1024

NEG_BIG = -1e30
LOG2E = math.log2(math.e)

_F32 = jnp.float32
_BF16 = jnp.bfloat16


def _cparams(n_axes):
    return pltpu.CompilerParams(dimension_semantics=("arbitrary",) * n_axes,
                                vmem_limit_bytes=VMEM_LIMIT)


def _dot(a, b):
    return jnp.dot(a, b, preferred_element_type=_F32)


def _dot_nt(a, b, precision=None):
    return lax.dot_general(a, b, (((1,), (1,)), ((), ())), precision=precision,
                           preferred_element_type=_F32)


def _const_spec(shape):
    zeros = (0,) * len(shape)
    return pl.BlockSpec(shape, lambda *_: zeros, pipeline_mode=pl.Buffered(1))


def _rope_table_kernel(pos_ref, inv_ref, sel_ref, out_ref):
    pos = pos_ref[...].astype(_F32)
    ang = pos * inv_ref[...]
    c, s = jnp.cos(ang), jnp.sin(ang)
    sel = sel_ref[...]
    out_ref[0] = c * (sel[0:1] + sel[1:2]) + sel[2:3]
    out_ref[1] = -s * sel[0:1]
    out_ref[2] = s * sel[1:2]


def _rope_tables(pos_col, inv_row, sel_rows, tm):
    t = pos_col.shape[0]
    return pl.pallas_call(
        _rope_table_kernel,
        grid=(t // tm,),
        in_specs=[pl.BlockSpec((tm, 1), lambda i: (i, 0)), _const_spec((1, LANE)), _const_spec((3, LANE))],
        out_specs=pl.BlockSpec((3, tm, LANE), lambda i: (0, i, 0)),
        out_shape=jax.ShapeDtypeStruct((3, t, LANE), _F32),
        compiler_params=_cparams(1),
        name="rope_tables",
    )(pos_col, inv_row, sel_rows)


def _rope_lane_patterns():
    lane = jnp.arange(LANE)
    half_a = MLA_ROPE // 2
    in_rope = (lane >= MLA_NOPE) & (lane < MLA_NOPE + MLA_ROPE)
    fi = (lane - MLA_NOPE) % half_a
    inv_a = jnp.where(in_rope, ROPE_THETA ** (-(2.0 * fi.astype(_F32)) / MLA_ROPE), 0.0)
    x1_a = in_rope & (lane < MLA_NOPE + half_a)
    x2_a = in_rope & ~x1_a
    sel_a = jnp.stack([x1_a, x2_a, ~in_rope]).astype(_F32)
    half_d = DIFF_ROT // 2
    d = lane % DIFF_HEAD_DIM
    in_rot = d < DIFF_ROT
    inv_d = jnp.where(in_rot, ROPE_THETA ** (-(2.0 * (d % half_d).astype(_F32)) / DIFF_ROT), 0.0)
    x1_d = d < half_d
    x2_d = in_rot & ~x1_d
    sel_d = jnp.stack([x1_d, x2_d, ~in_rot]).astype(_F32)
    return (inv_a[None].astype(_F32), sel_a), (inv_d[None].astype(_F32), sel_d)


def _rope_chunk(x, tab_ref, half):
    return (x * tab_ref[0] + pltpu.roll(x, LANE - half, 1) * tab_ref[1]
            + pltpu.roll(x, half, 1) * tab_ref[2])


def _rms(x, g, eps):
    return x * lax.rsqrt(jnp.mean(x * x, axis=-1, keepdims=True) + eps) * g


def _inproj_kernel(x_ref, taba_ref, tabd_ref, w1_ref, qn_ref, wuq_ref, kvn_ref, wk_ref, wv_ref, bg_ref,
                   qmt_ref, km_ref, vmt_ref, qdt_ref, kd_ref, vdt_ref, us_ref, gates_ref,
                   *, q_scale_mla, q_scale_diff, d_model):
    xb = x_ref[...].astype(_BF16)
    n_mla = MLA_HEADS
    lat = _dot(xb, w1_ref[:, 0:4 * LANE])
    cqn = _rms(lat[:, :MLA_Q_LORA], qn_ref[...], RMS_EPS).astype(_BF16)
    q = _dot(cqn, wuq_ref[...])
    for h in range(n_mla):
        sl = slice(h * LANE, (h + 1) * LANE)
        qh = _rope_chunk(q[:, sl], taba_ref, MLA_ROPE // 2) * q_scale_mla
        qmt_ref[0, sl, :] = qh.T.astype(_BF16)
    ckvn = _rms(lat[:, MLA_Q_LORA:MLA_Q_LORA + MLA_KV_LORA], kvn_ref[...], RMS_EPS).astype(_BF16)
    kn = _dot(ckvn, wk_ref[...])
    kr = _rope_chunk(lat[:, 3 * LANE:4 * LANE], taba_ref, MLA_ROPE // 2)
    for h in range(n_mla):
        sl = slice(h * LANE, (h + 1) * LANE)
        km_ref[:, sl] = (kn[:, sl] + kr).astype(_BF16)
    vm = _dot(ckvn, wv_ref[...])
    seg = DIFF_HEADS * 2 * DIFF_HEAD_DIM
    for c in range(seg // LANE):
        sl = slice(c * LANE, (c + 1) * LANE)
        vmt_ref[0, sl, :] = vm[:, sl].T.astype(_BF16)

    base = 4 * LANE
    lane = lax.broadcasted_iota(jnp.int32, (1, LANE), 1)
    comp = lane // DIFF_HEAD_DIM
    qd = _dot(xb, w1_ref[:, base:base + seg])
    kd = _dot(xb, w1_ref[:, base + seg:base + 2 * seg])
    vd = _dot(xb, w1_ref[:, base + 2 * seg:base + 3 * seg])
    for c in range(seg // LANE):
        sl = slice(c * LANE, (c + 1) * LANE)
        qc = _rope_chunk(qd[:, sl], tabd_ref, DIFF_ROT // 2) * q_scale_diff
        qdt_ref[0, sl, :] = qc.T.astype(_BF16)
        kc = _rope_chunk(kd[:, sl], tabd_ref, DIFF_ROT // 2)
        for v in range(4):
            kd_ref[v, :, sl] = jnp.where(comp == v, kc, 0.0).astype(_BF16)
        vdt_ref[0, sl, :] = vd[:, sl].T.astype(_BF16)
    us_ref[...] = _dot(xb, w1_ref[:, base + 3 * seg:base + 4 * seg]).astype(_BF16)
    gbase = base + 4 * seg
    for c in range(N_BRANCH):
        sl = slice(c * d_model, (c + 1) * d_model)
        g = _dot(xb, w1_ref[:, gbase + c * d_model:gbase + (c + 1) * d_model]) + bg_ref[:, sl]
        gates_ref[:, sl] = jax.nn.sigmoid(g).astype(_BF16)


def _inproj(x2d, taba, tabd, w1, qn, wuq, kvn, wk, wv, bg, tm):
    t, d_model = x2d.shape
    nt = t // tm
    seg = DIFF_HEADS * 2 * DIFF_HEAD_DIM
    hm = MLA_HEADS * LANE
    row = lambda w: pl.BlockSpec((tm, w), lambda i: (i, 0))
    colt = lambda w: pl.BlockSpec((1, w, tm), lambda i: (i, 0, 0))
    tab = pl.BlockSpec((3, tm, LANE), lambda i: (0, i, 0))
    kern = functools.partial(
        _inproj_kernel,
        q_scale_mla=float((MLA_NOPE + MLA_ROPE) ** -0.5 * LOG2E),
        q_scale_diff=float(DIFF_HEAD_DIM ** -0.5 * LOG2E),
        d_model=d_model)
    out_shapes = (
        jax.ShapeDtypeStruct((nt, hm, tm), _BF16), jax.ShapeDtypeStruct((t, hm), _BF16),
        jax.ShapeDtypeStruct((nt, MLA_HEADS * MLA_V, tm), _BF16),
        jax.ShapeDtypeStruct((nt, seg, tm), _BF16), jax.ShapeDtypeStruct((4, t, seg), _BF16),
        jax.ShapeDtypeStruct((nt, seg, tm), _BF16), jax.ShapeDtypeStruct((t, seg), _BF16),
        jax.ShapeDtypeStruct((t, N_BRANCH * d_model), _BF16))
    out_specs = (colt(hm), row(hm), colt(MLA_HEADS * MLA_V), colt(seg),
                 pl.BlockSpec((4, tm, seg), lambda i: (0, i, 0)), colt(seg), row(seg), row(N_BRANCH * d_model))
    return pl.pallas_call(
        kern,
        grid=(nt,),
        in_specs=[row(d_model), tab, tab, _const_spec(w1.shape), _const_spec(qn.shape), _const_spec(wuq.shape),
                  _const_spec(kvn.shape), _const_spec(wk.shape), _const_spec(wv.shape), _const_spec(bg.shape)],
        out_specs=out_specs,
        out_shape=out_shapes,
        compiler_params=_cparams(1),
        name="inproj",
    )(x2d, taba, tabd, w1, qn, wuq, kvn, wk, wv, bg)


def _flash_maps(q_ts, load_ks, load_vt, qi, tq, s_scrs):
    n = len(q_ts)
    a_scrs = s_scrs[n:]

    def qk(i, j):
        s_scrs[i][...] = _dot(load_ks[i](j), q_ts[i])

    def step(j, carry, diagonal):
        vt = load_vt(j)
        if diagonal:
            r = lax.broadcasted_iota(jnp.int32, (tq, tq), 0)
            c = lax.broadcasted_iota(jnp.int32, (tq, tq), 1)
            keep = r <= c
        out = []
        for i in range(n):
            if i + 1 < n:
                qk(i + 1, j)
            elif not diagonal:
                qk(0, j + 1)
            m, l = carry[i]
            s = s_scrs[i][...]
            if diagonal:
                s = jnp.where(keep, s, NEG_BIG)
            m_new = jnp.maximum(m, jnp.max(s, axis=0, keepdims=True))
            alpha = jnp.exp2(m - m_new)
            p = jnp.exp2(s - m_new)
            l = alpha * l + jnp.sum(p, axis=0, keepdims=True)
            a_scrs[i][...] = alpha * a_scrs[i][...] + _dot(vt, p.astype(_BF16))
            out.append((m_new, l))
        return tuple(out)

    init = tuple((jnp.full((1, tq), NEG_BIG, _F32), jnp.zeros((1, tq), _F32)) for _ in range(n))
    for i in range(n):
        a_scrs[i][...] = jnp.zeros_like(a_scrs[i])
    qk(0, 0)
    carry = lax.fori_loop(0, qi, lambda j, c: step(j, c, False), init)
    carry = step(qi, carry, True)
    return [a_scrs[i][...] / carry[i][1] for i in range(n)]


def _key_block(k_ref, lead, j, tq, lanes):
    return k_ref[lead + (pl.ds(pl.multiple_of(j * tq, tq), tq), lanes)]


def _mla_attn_kernel(qt_ref, k_ref, vt_ref, o_ref, *s_scrs, tq):
    qi = pl.program_id(2)
    sls = [slice(hh * LANE, (hh + 1) * LANE) for hh in range(2)]
    outs = _flash_maps([qt_ref[0, sl, :] for sl in sls],
                       [lambda j, sl=sl: _key_block(k_ref, (0,), j, tq, sl) for sl in sls],
                       lambda j: vt_ref[0, j], qi, tq, s_scrs)
    ot = jnp.concatenate([outs[0][:MLA_V], outs[1][MLA_V:]], axis=0)
    o_ref[0] = ot.T.astype(o_ref.dtype)


def _mla_attention(qt, k, vt, b, tq):
    s = k.shape[1]
    nq = s // tq
    pairs = MLA_HEADS // 2
    vt4 = vt.reshape(b, nq, vt.shape[1], tq)
    return pl.pallas_call(
        functools.partial(_mla_attn_kernel, tq=tq),
        grid=(b, pairs, nq),
        in_specs=[pl.BlockSpec((1, 2 * LANE, tq), lambda bi, p, qi: (bi * nq + qi, p, 0)),
                  pl.BlockSpec((1, s, 2 * LANE), lambda bi, p, qi: (bi, 0, p)),
                  pl.BlockSpec((1, nq, LANE, tq), lambda bi, p, qi: (bi, 0, p, 0))],
        out_specs=pl.BlockSpec((1, tq, LANE), lambda bi, p, qi: (bi, qi, p)),
        out_shape=jax.ShapeDtypeStruct((b, s, pairs * LANE), _BF16),
        scratch_shapes=[pltpu.VMEM((tq, tq), _F32)] * 2 + [pltpu.VMEM((LANE, tq), _F32)] * 2,
        compiler_params=_cparams(3),
        name="mla_attention",
    )(qt, k, vt4)


def _diff_attn_kernel(qt_ref, k_ref, vt_ref, lam_ref, subln_ref, o_ref, *s_scrs, tq, lambda_init):
    qi = pl.program_id(2)
    qt = qt_ref[0]
    outs = _flash_maps([qt] * 4,
                       [lambda j, v=v: _key_block(k_ref, (v, 0), j, tq, slice(None)) for v in range(4)],
                       lambda j: vt_ref[0, j], qi, tq, s_scrs)
    lf = lam_ref[...]
    lam = (jnp.exp(jnp.sum(lf[0:1] * lf[1:2], axis=-1, keepdims=True))
           - jnp.exp(jnp.sum(lf[2:3] * lf[3:4], axis=-1, keepdims=True)) + lambda_init)
    half = 2 * DIFF_HEAD_DIM
    dt = jnp.concatenate([(outs[0] - lam * outs[1])[:half], (outs[2] - lam * outs[3])[half:]], axis=0)
    d = dt.T
    lane = lax.broadcasted_iota(jnp.int32, (tq, LANE), 1)
    first = lane < half
    sq = d * d
    ss_a = jnp.sum(jnp.where(first, sq, 0.0), axis=-1, keepdims=True)
    ss_b = jnp.sum(jnp.where(first, 0.0, sq), axis=-1, keepdims=True)
    ms = jnp.where(first, ss_a, ss_b) * (1.0 / half)
    o_ref[0] = (d * lax.rsqrt(ms + SUBLN_EPS) * subln_ref[...] * (1.0 - lambda_init)).astype(o_ref.dtype)


def _diff_attention(qt, kvar, vt, lam, subln2, b, tq, lambda_init):
    s = kvar.shape[2]
    nq = s // tq
    pairs = DIFF_HEADS // 2
    vt4 = vt.reshape(b, nq, vt.shape[1], tq)
    return pl.pallas_call(
        functools.partial(_diff_attn_kernel, tq=tq, lambda_init=lambda_init),
        grid=(b, pairs, nq),
        in_specs=[pl.BlockSpec((1, LANE, tq), lambda bi, p, qi: (bi * nq + qi, p, 0)),
                  pl.BlockSpec((4, 1, s, LANE), lambda bi, p, qi: (0, bi, 0, p)),
                  pl.BlockSpec((1, nq, LANE, tq), lambda bi, p, qi: (bi, 0, p, 0)),
                  _const_spec(lam.shape), _const_spec(subln2.shape)],
        out_specs=pl.BlockSpec((1, tq, LANE), lambda bi, p, qi: (bi, qi, p)),
        out_shape=jax.ShapeDtypeStruct((b, s, pairs * LANE), _BF16),
        scratch_shapes=[pltpu.VMEM((tq, tq), _F32)] * 4 + [pltpu.VMEM((LANE, tq), _F32)] * 4,
        compiler_params=_cparams(3),
        name="diff_attention",
    )(qt, kvar, vt4, lam, subln2)


def _ssm_kernel(u_ref, t_ref, b_ref, c_ref, ar_ref, ai_ref, d_ref, y_ref, x_scr, h_scr, *, batch, chunks):
    u = u_ref[0]
    x_scr[...] = _dot(u, b_ref[0])
    ar, ai = ar_ref[0], ai_ref[0]

    def step(c, h):
        r0 = c * batch
        if batch % 8 == 0:
            r0 = pl.multiple_of(r0, 8)
        h_scr[pl.ds(r0, batch), :] = h
        return ar * h + ai * pltpu.roll(h, SSM_STATE, 1) + x_scr[pl.ds(r0, batch), :]

    lax.fori_loop(0, chunks, step, jnp.zeros((batch, 2 * SSM_STATE), _F32))
    y = _dot(u, t_ref[0]) + _dot(h_scr[...].astype(_BF16), c_ref[0]) + u.astype(_F32) * d_ref[0]
    y_ref[0] = jax.nn.gelu(y).astype(y_ref.dtype)


def _ssm(u_g, tmat, bmat, cmat, ar, ai, dflat, batch):
    g, r, w = u_g.shape
    per_g = lambda shape: pl.BlockSpec((1,) + shape, lambda i: (i, 0, 0))
    return pl.pallas_call(
        functools.partial(_ssm_kernel, batch=batch, chunks=r // batch),
        grid=(g,),
        in_specs=[per_g((r, w)), per_g((w, w)), per_g((w, 2 * SSM_STATE)), per_g((2 * SSM_STATE, w)),
                  per_g((1, 2 * SSM_STATE)), per_g((1, 2 * SSM_STATE)), per_g((1, w))],
        out_specs=per_g((r, w)),
        out_shape=jax.ShapeDtypeStruct((g, r, w), _BF16),
        scratch_shapes=[pltpu.VMEM((r, 2 * SSM_STATE), _F32), pltpu.VMEM((r, 2 * SSM_STATE), _F32)],
        compiler_params=_cparams(1),
        name="ssm_scan",
    )(u_g, tmat, bmat, cmat, ar, ai, dflat)


def _ssm_matrices(a_re, a_im, log_dt, b_re, b_im, c_re, c_im, d):
    L, P, N = SSM_CHUNK, SSM_GROUP, SSM_STATE
    G = a_re.shape[0]
    A = lax.complex(a_re.astype(_F32), a_im.astype(_F32))
    dt = jnp.exp(log_dt.astype(_F32))[:, None]
    a_bar = jnp.exp(A * dt)
    b_bar = ((a_bar - 1.0) / A)[..., None] * lax.complex(b_re.astype(_F32), b_im.astype(_F32))
    C = lax.complex(c_re.astype(_F32), c_im.astype(_F32))
    steps = jnp.arange(L + 1, dtype=_F32)
    apow = jnp.exp((A * dt)[None] * steps[:, None, None])
    m = jnp.real(jnp.einsum('gpn,dgn,gnq->dgpq', C, apow[:L], b_bar))
    s_idx = jnp.arange(L)[:, None]
    t_idx = jnp.arange(L)[None, :]
    lag = jnp.clip(t_idx - s_idx, 0, L - 1)
    tfull = jnp.where((t_idx >= s_idx)[:, :, None, None, None], m[lag], 0.0)
    tmat = tfull.transpose(2, 0, 4, 1, 3).reshape(G, L * P, L * P)
    bm = apow[L - 1 - jnp.arange(L)][:, :, :, None] * b_bar[None]
    bm = bm.transpose(1, 0, 3, 2).reshape(G, L * P, N)
    bmat = jnp.concatenate([jnp.real(bm), jnp.imag(bm)], axis=-1)
    cm = C[None] * apow[1:L + 1][:, :, None, :]
    cm = cm.transpose(1, 3, 0, 2).reshape(G, N, L * P)
    cmat = jnp.concatenate([jnp.real(cm), -jnp.imag(cm)], axis=1)
    al = apow[L]
    ar = jnp.concatenate([jnp.real(al), jnp.real(al)], axis=-1)[:, None, :]
    ai = jnp.concatenate([-jnp.imag(al), jnp.imag(al)], axis=-1)[:, None, :]
    dflat = jnp.tile(d.astype(_F32).reshape(G, 1, P), (1, L, 1)).reshape(G, 1, L * P)
    return tmat.astype(_BF16), bmat.astype(_BF16), cmat.astype(_BF16), ar, ai, dflat


def _layernorm(z, g, b):
    mu = jnp.mean(z, axis=-1, keepdims=True)
    zc = z - mu
    var = jnp.mean(zc * zc, axis=-1, keepdims=True)
    return zc * lax.rsqrt(var + LN_EPS) * g + b


def _merge_kernel(om_ref, od_ref, ys_ref, g_ref, x_ref, wmo_ref, wdo_ref, wglu_ref, wo_ref, lng_ref, lnb_ref,
                  rw_ref, rb_ref, x1_ref, idx_ref, wts_ref, cnt_ref, *, alpha, d_model):
    def branch_dots(rows):
        return (_dot(om_ref[rows, :], wmo_ref[...]), _dot(od_ref[rows, :], wdo_ref[...]),
                _dot(ys_ref[rows, :], wglu_ref[...]))

    def gate_merge(rows, ya, yb, gl):
        yc = gl[:, :d_model] * jax.nn.sigmoid(gl[:, d_model:])
        merged = (g_ref[rows, 0:d_model].astype(_F32) * ya + g_ref[rows, d_model:2 * d_model].astype(_F32) * yb
                  + g_ref[rows, 2 * d_model:3 * d_model].astype(_F32) * yc)
        return merged.astype(_BF16)

    def out_norm(rows, merged):
        z = alpha * x_ref[rows, :] + _dot(merged, wo_ref[...])
        x1 = _layernorm(z, lng_ref[...], lnb_ref[...])
        for c in range(d_model // LANE):
            x1_ref[pl.ds(rows.start * TOKEN_TILE + c, rows.stop - rows.start, stride=TOKEN_TILE), :] = (
                x1[:, c * LANE:(c + 1) * LANE])
        return x1

    def router_logits(x1):
        return _dot_nt(rw_ref[...], x1, precision=lax.Precision.HIGHEST)

    def route(rows, logits):
        scores = jax.nn.sigmoid(logits)
        biased = scores + rb_ref[...]
        epg = logits.shape[0] // N_GROUPS
        a = [biased[j * N_GROUPS:(j + 1) * N_GROUPS] for j in range(epg)]
        sc = [scores[j * N_GROUPS:(j + 1) * N_GROUPS] for j in range(epg)]
        gs = None
        for i in range(epg):
            for j in range(i + 1, epg):
                pair = a[i] + a[j]
                gs = pair if gs is None else jnp.maximum(gs, pair)
        giota = lax.broadcasted_iota(jnp.int32, gs.shape, 0)
        gmax = jnp.max(gs, axis=0, keepdims=True)
        gsel = jnp.min(jnp.where(gs == gmax, giota, N_GROUPS), axis=0, keepdims=True)
        hot = giota == gsel
        val = [jnp.sum(jnp.where(hot, a[j], 0.0), axis=0, keepdims=True) for j in range(epg)]
        raw = [jnp.sum(jnp.where(hot, sc[j], 0.0), axis=0, keepdims=True) for j in range(epg)]
        b1, i1, w1 = val[0], jnp.zeros_like(gsel), raw[0]
        for j in range(1, epg):
            take = val[j] > b1
            b1 = jnp.where(take, val[j], b1)
            i1 = jnp.where(take, j, i1)
            w1 = jnp.where(take, raw[j], w1)
        b2 = jnp.full_like(b1, -jnp.inf)
        i2 = jnp.zeros_like(gsel)
        w2 = jnp.zeros_like(w1)
        for j in range(epg):
            take = (i1 != j) & (val[j] > b2)
            b2 = jnp.where(take, val[j], b2)
            i2 = jnp.where(take, j, i2)
            w2 = jnp.where(take, raw[j], w2)
        wsum = w1 + w2
        e1, e2 = gsel * epg + i1, gsel * epg + i2
        idx_ref[:, rows] = jnp.concatenate([e1, e2], axis=0)
        wts_ref[:, rows] = jnp.concatenate([w1 / wsum, w2 / wsum], axis=0)
        eiota = lax.broadcasted_iota(jnp.int32, logits.shape, 0)
        hit = (eiota == e1).astype(jnp.int32) + (eiota == e2).astype(jnp.int32)
        part = hit[:, 0:LANE]
        for c in range(1, hit.shape[1] // LANE):
            part = part + hit[:, c * LANE:(c + 1) * LANE]
        return part

    tm = x_ref.shape[0]
    sub = tm // MERGE_SUBTILES
    rows = [slice(k * sub, (k + 1) * sub) for k in range(MERGE_SUBTILES)]
    dots = [branch_dots(r) for r in rows]
    merged = [gate_merge(r, *d) for r, d in zip(rows, dots)]
    x1s = [out_norm(r, m) for r, m in zip(rows, merged)]
    logits = [router_logits(x1) for x1 in x1s]
    parts = [route(r, lg) for r, lg in zip(rows, logits)]

    @pl.when(pl.program_id(0) == 0)
    def _():
        cnt_ref[...] = jnp.zeros_like(cnt_ref)

    cnt_ref[...] += functools.reduce(lambda p, q: p + q, parts)


def _merge(om, od, ys, gates, x2d, wmo, wdo, wglu, wo, lng, lnb, rw, rb, tm, alpha):
    t, d_model = x2d.shape
    n_experts = rw.shape[0]
    assert d_model == TOKEN_TILE * LANE and tm % MERGE_SUBTILES == 0
    row = lambda w: pl.BlockSpec((tm, w), lambda i: (i, 0))
    col = pl.BlockSpec((TOP_K, tm), lambda i: (0, i))
    return pl.pallas_call(
        functools.partial(_merge_kernel, alpha=alpha, d_model=d_model),
        grid=(t // tm,),
        in_specs=[row(om.shape[1]), row(od.shape[1]), row(ys.shape[1]), row(gates.shape[1]), row(d_model),
                  _const_spec(wmo.shape), _const_spec(wdo.shape), _const_spec(wglu.shape), _const_spec(wo.shape),
                  _const_spec(lng.shape), _const_spec(lnb.shape), _const_spec(rw.shape), _const_spec(rb.shape)],
        out_specs=(pl.BlockSpec((tm * TOKEN_TILE, LANE), lambda i: (i, 0)), col, col,
                   pl.BlockSpec((n_experts, LANE), lambda i: (0, 0))),
        out_shape=(jax.ShapeDtypeStruct((t * TOKEN_TILE, LANE), _F32), jax.ShapeDtypeStruct((TOP_K, t), jnp.int32),
                   jax.ShapeDtypeStruct((TOP_K, t), _F32), jax.ShapeDtypeStruct((n_experts, LANE), jnp.int32)),
        compiler_params=_cparams(1),
        name="merge_router",
    )(om, od, ys, gates, x2d, wmo, wdo, wglu, wo, lng, lnb, rw, rb)


DMA_UNROLL = 8


def _moe_kernel(be_ref, cnt_ref, nu_ref, tok_ref, tokn_ref, dst_ref, x_hbm, wg_ref, wu_ref, wd_ref, out_hbm,
                xbuf, ybuf, wgb, wub, wdb, sem_in, sem_out, *, mb):
    i = pl.program_id(0)
    n_used = nu_ref[0]
    slot = lax.rem(i, 2)

    def for_rows(start_row):
        def body(j, carry):
            for u in range(DMA_UNROLL):
                start_row(j * DMA_UNROLL + u, u % 2)
            return carry
        lax.fori_loop(0, mb // DMA_UNROLL, body, 0)

    def for_first_rows(n, fn):
        def body(r, carry):
            fn(r)
            return carry
        lax.fori_loop(0, n, body, 0)

    def tile_rows(tok):
        return pl.ds(pl.multiple_of(tok * TOKEN_TILE, TOKEN_TILE), TOKEN_TILE)

    def gather_row(t_ref, s, r):
        return pltpu.make_async_copy(x_hbm.at[tile_rows(t_ref[0, 0, r]), :], xbuf.at[s, tile_rows(r), :],
                                     sem_in.at[s])

    def scatter_row(s, r, dst):
        return pltpu.make_async_copy(ybuf.at[s, tile_rows(r), :], out_hbm.at[tile_rows(dst), :], sem_out.at[s])

    def issue_gather(t_ref, s):
        for_rows(lambda r, pr: gather_row(t_ref, s, r).start(priority=pr))

    def wait_gather(s):
        pltpu.make_async_copy(x_hbm.at[pl.ds(0, mb * TOKEN_TILE), :], xbuf.at[s], sem_in.at[s]).wait()

    def issue_scatter(s, cnt):
        @pl.when(cnt == mb)
        def _():
            for_rows(lambda r, pr: scatter_row(s, r, dst_ref[0, 0, r]).start(priority=pr))

        @pl.when(cnt < mb)
        def _():
            for_first_rows(cnt, lambda r: scatter_row(s, r, dst_ref[0, 0, r]).start())

    def wait_scatter(s, cnt):
        @pl.when(cnt == mb)
        def _():
            pltpu.make_async_copy(ybuf.at[s], out_hbm.at[pl.ds(0, mb * TOKEN_TILE), :], sem_out.at[s]).wait()

        @pl.when(cnt < mb)
        def _():
            for_first_rows(cnt, lambda r: scatter_row(s, r, 0).wait())

    @pl.when(i < n_used)
    def _():
        @pl.when(i == 0)
        def _():
            issue_gather(tok_ref, 0)

        @pl.when(i + 1 < n_used)
        def _():
            issue_gather(tokn_ref, 1 - slot)

        @pl.when((i == 0) | (be_ref[i] != be_ref[jnp.maximum(i - 1, 0)]))
        def _():
            wgb[...] = wg_ref[0, 0].astype(_BF16)
            wub[...] = wu_ref[0, 0].astype(_BF16)
            wdb[...] = wd_ref[0, 0].astype(_BF16)

        wait_gather(slot)

        @pl.when(i >= 2)
        def _():
            wait_scatter(slot, cnt_ref[jnp.maximum(i - 2, 0)])

        xb = jnp.concatenate([xbuf[slot, pl.ds(c, mb, stride=TOKEN_TILE), :] for c in range(TOKEN_TILE)],
                             axis=1).astype(_BF16)
        h = jax.nn.silu(_dot(xb, wgb[...])) * _dot(xb, wub[...])
        y = _dot(h.astype(_BF16), wdb[...])
        for c in range(TOKEN_TILE):
            ybuf[slot, pl.ds(c, mb, stride=TOKEN_TILE), :] = y[:, c * LANE:(c + 1) * LANE]
        issue_scatter(slot, cnt_ref[i])

        @pl.when(i == n_used - 1)
        def _():
            @pl.when(i >= 1)
            def _():
                wait_scatter(1 - slot, cnt_ref[jnp.maximum(i - 1, 0)])
            wait_scatter(slot, cnt_ref[i])


def _moe(block_e, block_cnt, n_used, row_tok, row_dst, x1t, wg, wu, wd, layer):
    t = x1t.shape[0] // TOKEN_TILE
    d_model = TOKEN_TILE * LANE
    n_blocks, mb = row_tok.shape
    ff = wg.shape[-1]
    assert mb % DMA_UNROLL == 0
    smem_row = lambda f: pl.BlockSpec((1, 1, mb), lambda i, be, cnt, nu: (f(i), 0, 0), memory_space=pltpu.SMEM)
    wspec = lambda a, b: pl.BlockSpec((1, 1, a, b), lambda i, be, cnt, nu: (layer, be[i], 0, 0))
    grid_spec = pltpu.PrefetchScalarGridSpec(
        num_scalar_prefetch=3,
        grid=(n_blocks,),
        in_specs=[smem_row(lambda i: i), smem_row(lambda i: jnp.minimum(i + 1, n_blocks - 1)), smem_row(lambda i: i),
                  pl.BlockSpec(memory_space=pl.ANY),
                  wspec(d_model, ff), wspec(d_model, ff), wspec(ff, d_model)],
        out_specs=pl.BlockSpec(memory_space=pl.ANY),
        scratch_shapes=[pltpu.VMEM((2, mb * TOKEN_TILE, LANE), _F32), pltpu.VMEM((2, mb * TOKEN_TILE, LANE), _F32),
                        pltpu.VMEM((d_model, ff), _BF16), pltpu.VMEM((d_model, ff), _BF16),
                        pltpu.VMEM((ff, d_model), _BF16),
                        pltpu.SemaphoreType.DMA((2,)), pltpu.SemaphoreType.DMA((2,))])
    tok3 = row_tok.reshape(n_blocks, 1, mb)
    return pl.pallas_call(
        functools.partial(_moe_kernel, mb=mb),
        grid_spec=grid_spec,
        out_shape=jax.ShapeDtypeStruct((TOP_K * t * TOKEN_TILE, LANE), _F32),
        compiler_params=_cparams(1),
        name="moe_experts",
    )(block_e, block_cnt, n_used, tok3, tok3, row_dst.reshape(n_blocks, 1, mb), x1t, wg, wu, wd)


def _moe_plan(idx, counts, mb):
    k, t = idx.shape
    a_total = k * t
    n_experts = counts.shape[0]
    e_flat = idx.reshape(a_total)
    order = jnp.argsort(e_flat).astype(jnp.int32)
    starts = jnp.cumsum(counts) - counts
    padded = (counts + mb - 1) // mb * mb
    pad_ends = jnp.cumsum(padded)
    pad_starts = pad_ends - padded
    n_blocks = -(-a_total // mb) + n_experts
    blk = jnp.arange(n_blocks, dtype=jnp.int32)
    blk_start = blk * mb
    block_e = jnp.minimum(jnp.sum((pad_ends[None, :] <= blk_start[:, None]).astype(jnp.int32), axis=1),
                          n_experts - 1)
    off0 = blk_start - pad_starts[block_e]
    cnt = jnp.clip(counts[block_e] - off0, 0, mb)
    n_used = jnp.sum((cnt > 0).astype(jnp.int32)).reshape(1)
    lane = jnp.arange(mb, dtype=jnp.int32)[None, :]
    valid = lane < cnt[:, None]
    src = jnp.clip((starts[block_e] + off0)[:, None] + lane, 0, a_total - 1)
    a_r = order[src]
    row_dst = jnp.where(valid, a_r, 0).astype(jnp.int32)
    row_tok = jnp.where(valid, a_r % t, 0).astype(jnp.int32)
    return block_e.astype(jnp.int32), cnt.astype(jnp.int32), n_used, row_tok, row_dst


def _final_kernel(x_ref, y0_ref, y1_ref, w_ref, lng_ref, lnb_ref, o_ref, *, alpha):
    tm = o_ref.shape[0]

    def tokens(ref):
        return jnp.concatenate([ref[pl.ds(c, tm, stride=TOKEN_TILE), :] for c in range(TOKEN_TILE)], axis=1)

    w = w_ref[...]
    z = alpha * tokens(x_ref) + (tokens(y0_ref) * w[:, 0:1] + tokens(y1_ref) * w[:, 1:2])
    o_ref[...] = _layernorm(z, lng_ref[...], lnb_ref[...])


def _final(x1t, yslots, wts_t, lng, lnb, tm, alpha):
    t = x1t.shape[0] // TOKEN_TILE
    d_model = TOKEN_TILE * LANE
    nt = t // tm
    tile = lambda f: pl.BlockSpec((tm * TOKEN_TILE, LANE), lambda i: (f(i), 0))
    return pl.pallas_call(
        functools.partial(_final_kernel, alpha=alpha),
        grid=(nt,),
        in_specs=[tile(lambda i: i), tile(lambda i: i), tile(lambda i: nt + i),
                  pl.BlockSpec((tm, TOP_K), lambda i: (i, 0)),
                  _const_spec(lng.shape), _const_spec(lnb.shape)],
        out_specs=pl.BlockSpec((tm, d_model), lambda i: (i, 0)),
        out_shape=jax.ShapeDtypeStruct((t, d_model), _F32),
        compiler_params=_cparams(1),
        name="moe_residual_ln",
    )(x1t, yslots, yslots, wts_t, lng, lnb)


def _prep_w_in(w):
    d_model = w.shape[0]
    seg = DIFF_HEADS * 2 * DIFF_HEAD_DIM
    o = 0
    cq = w[:, o:o + MLA_Q_LORA]; o += MLA_Q_LORA
    ckv = w[:, o:o + MLA_KV_LORA]; o += MLA_KV_LORA
    kr = w[:, o:o + MLA_ROPE]; o += MLA_ROPE
    rest = w[:, o:]
    z = lambda n: jnp.zeros((d_model, n), w.dtype)
    kr_chunk = jnp.concatenate([z(MLA_NOPE), kr, z(LANE - MLA_NOPE - MLA_ROPE)], axis=1)
    assert MLA_Q_LORA + MLA_KV_LORA + LANE == 4 * LANE and rest.shape[1] == 4 * seg + N_BRANCH * d_model
    return jnp.concatenate([cq, ckv, kr_chunk, rest], axis=1).astype(_BF16)


def _prep_w_uq(w):
    dq = MLA_NOPE + MLA_ROPE
    w3 = w.reshape(w.shape[0], MLA_HEADS, dq)
    w3 = jnp.pad(w3, ((0, 0), (0, 0), (0, LANE - dq)))
    return w3.reshape(w.shape[0], MLA_HEADS * LANE).astype(_BF16)


def _prep_w_ukv(w):
    w3 = w.reshape(w.shape[0], MLA_HEADS, MLA_NOPE + MLA_V)
    wk = jnp.pad(w3[:, :, :MLA_NOPE], ((0, 0), (0, 0), (0, LANE - MLA_NOPE)))
    wv = w3[:, :, MLA_NOPE:]
    return (wk.reshape(w.shape[0], MLA_HEADS * LANE).astype(_BF16),
            wv.reshape(w.shape[0], MLA_HEADS * MLA_V).astype(_BF16))


def kernel(x, positions, w_in, b_gate, mla_q_norm, mla_w_uq, mla_kv_norm, mla_w_ukv, mla_w_out,
           diff_lambda, diff_subln, diff_w_out, ssm_a_re, ssm_a_im, ssm_log_dt, ssm_b_re, ssm_b_im,
           ssm_c_re, ssm_c_im, ssm_d, ssm_w_glu, w_o, ln_gain, ln_bias, router_w, router_bias,
           moe_w_gate, moe_w_up, moe_w_down):
    B, S, D = x.shape
    T = B * S
    depth = w_in.shape[0]
    n_experts = router_w.shape[1]
    epg = n_experts // N_GROUPS
    alpha = float((2 * depth) ** 0.25)
    tm = min(ROW_TILE, T)
    tq = min(ATTN_TILE, S)
    L, P = SSM_CHUNK, SSM_GROUP
    G = ssm_a_re.shape[1]
    assert T % tm == 0 and S % tq == 0 and S % L == 0 and (TOP_K * T) % MOE_ROWS == 0
    assert MLA_HEADS * MLA_V == DIFF_HEADS * 2 * DIFF_HEAD_DIM == G * P

    (inv_a, sel_a), (inv_d, sel_d) = _rope_lane_patterns()
    pos_col = positions.reshape(T, 1).astype(jnp.int32)
    taba = _rope_tables(pos_col, inv_a, sel_a, tm)
    tabd = _rope_tables(pos_col, inv_d, sel_d, tm)

    rw = router_w.T.reshape(N_GROUPS, epg, D).transpose(1, 0, 2).reshape(n_experts, D).astype(_F32)
    rb = router_bias.reshape(N_GROUPS, epg).T.reshape(n_experts, 1).astype(_F32)

    x2d = x.reshape(T, D)
    for l in range(depth):
        lambda_init = 0.8 - 0.6 * math.exp(-0.3 * l)
        w1 = _prep_w_in(w_in[l])
        wuq = _prep_w_uq(mla_w_uq[l])
        wk, wv = _prep_w_ukv(mla_w_ukv[l])
        qmt, km, vmt, qdt, kdv, vdt, us, gates = _inproj(
            x2d, taba, tabd, w1, mla_q_norm[l][None].astype(_F32), wuq, mla_kv_norm[l][None].astype(_F32),
            wk, wv, b_gate[l].reshape(1, N_BRANCH * D).astype(_F32), tq)

        o_mla = _mla_attention(qmt, km.reshape(B, S, -1), vmt, B, tq)
        subln2 = jnp.tile(diff_subln[l].astype(_F32), 2)[None]
        o_diff = _diff_attention(qdt, kdv.reshape(4, B, S, -1), vdt, diff_lambda[l].astype(_F32), subln2,
                                 B, tq, lambda_init)

        mats = _ssm_matrices(ssm_a_re[l], ssm_a_im[l], ssm_log_dt[l], ssm_b_re[l], ssm_b_im[l],
                             ssm_c_re[l], ssm_c_im[l], ssm_d[l])
        C = S // L
        u_g = us.reshape(B, C, L, G, P).transpose(3, 1, 0, 2, 4).reshape(G, C * B, L * P)
        y_g = _ssm(u_g, *mats, batch=B)
        ys = y_g.reshape(G, C, B, L, P).transpose(2, 1, 3, 0, 4).reshape(T, G * P)

        x1, idx, wts, cnt_part = _merge(
            o_mla.reshape(T, -1), o_diff.reshape(T, -1), ys, gates, x2d,
            mla_w_out[l].astype(_BF16), diff_w_out[l].astype(_BF16), ssm_w_glu[l].astype(_BF16),
            w_o[l].astype(_BF16), ln_gain[l, 0][None].astype(_F32), ln_bias[l, 0][None].astype(_F32),
            rw, rb, min(MERGE_TILE, T), alpha)

        plan = _moe_plan(idx, jnp.sum(cnt_part, axis=1), MOE_ROWS)
        yslots = _moe(*plan, x1, moe_w_gate.astype(_F32), moe_w_up.astype(_F32), moe_w_down.astype(_F32), l)
        x2d = _final(x1, yslots, wts.T, ln_gain[l, 1][None].astype(_F32), ln_bias[l, 1][None].astype(_F32),
                     tm, alpha)
    return x2d.reshape(B, S, D)
```

```python
import functools
import math

import jax
import jax.numpy as jnp
from jax import lax
from jax.experimental import pallas as pl
from jax.experimental.pallas import tpu as pltpu

MLA_HEADS = 8
MLA_Q_LORA = 256
MLA_KV_LORA = 128
MLA_NOPE = 64
MLA_ROPE = 32
MLA_V = 64
DIFF_HEADS = 8
DIFF_HEAD_DIM = 32
DIFF_ROT = DIFF_HEAD_DIM // 4
SSM_GROUP = 16
SSM_STATE = 64
N_BRANCH = 3
ROPE_THETA = 500000.0
N_GROUPS = 8
TOP_K = 2
LN_EPS = 1e-5
RMS_EPS = 1e-6
SUBLN_EPS = 1e-5

LANE = 128
ROW_TILE = 256
ATTN_TILE = 512
SSM_CHUNK = 16
SCAN_UNROLL = 4
MOE_ROWS = 256
MERGE_TILE = 512
MERGE_SUBTILES = 2
TOKEN_TILE = 8
VMEM_LIMIT = 56 * 1024 * 1024

NEG_BIG = -1e30
LOG2E = math.log2(math.e)

_F32 = jnp.float32
_BF16 = jnp.bfloat16


def _cparams(n_axes):
    return pltpu.CompilerParams(dimension_semantics=("arbitrary",) * n_axes,
                                vmem_limit_bytes=VMEM_LIMIT)


def _dot(a, b):
    return jnp.dot(a, b, preferred_element_type=_F32)


def _dot_nt(a, b, precision=None):
    return lax.dot_general(a, b, (((1,), (1,)), ((), ())), precision=precision,
                           preferred_element_type=_F32)


def _const_spec(shape):
    zeros = (0,) * len(shape)
    return pl.BlockSpec(shape, lambda *_: zeros, pipeline_mode=pl.Buffered(1))


def _rope_table_kernel(pos_ref, inv_ref, sel_ref, out_ref):
    pos = pos_ref[...].astype(_F32)
    ang = pos * inv_ref[...]
    c, s = jnp.cos(ang), jnp.sin(ang)
    sel = sel_ref[...]
    out_ref[0] = c * (sel[0:1] + sel[1:2]) + sel[2:3]
    out_ref[1] = -s * sel[0:1]
    out_ref[2] = s * sel[1:2]


def _rope_tables(pos_col, inv_row, sel_rows, tm):
    t = pos_col.shape[0]
    return pl.pallas_call(
        _rope_table_kernel,
        grid=(t // tm,),
        in_specs=[pl.BlockSpec((tm, 1), lambda i: (i, 0)), _const_spec((1, LANE)), _const_spec((3, LANE))],
        out_specs=pl.BlockSpec((3, tm, LANE), lambda i: (0, i, 0)),
        out_shape=jax.ShapeDtypeStruct((3, t, LANE), _F32),
        compiler_params=_cparams(1),
        name="rope_tables",
    )(pos_col, inv_row, sel_rows)


def _rope_lane_patterns():
    lane = jnp.arange(LANE)
    half_a = MLA_ROPE // 2
    in_rope = (lane >= MLA_NOPE) & (lane < MLA_NOPE + MLA_ROPE)
    fi = (lane - MLA_NOPE) % half_a
    inv_a = jnp.where(in_rope, ROPE_THETA ** (-(2.0 * fi.astype(_F32)) / MLA_ROPE), 0.0)
    x1_a = in_rope & (lane < MLA_NOPE + half_a)
    x2_a = in_rope & ~x1_a
    sel_a = jnp.stack([x1_a, x2_a, ~in_rope]).astype(_F32)
    half_d = DIFF_ROT // 2
    d = lane % DIFF_HEAD_DIM
    in_rot = d < DIFF_ROT
    inv_d = jnp.where(in_rot, ROPE_THETA ** (-(2.0 * (d % half_d).astype(_F32)) / DIFF_ROT), 0.0)
    x1_d = d < half_d
    x2_d = in_rot & ~x1_d
    sel_d = jnp.stack([x1_d, x2_d, ~in_rot]).astype(_F32)
    return (inv_a[None].astype(_F32), sel_a), (inv_d[None].astype(_F32), sel_d)


def _rope_chunk(x, tab_ref, half):
    return (x * tab_ref[0] + pltpu.roll(x, LANE - half, 1) * tab_ref[1]
            + pltpu.roll(x, half, 1) * tab_ref[2])


def _rms(x, g, eps):
    return x * lax.rsqrt(jnp.mean(x * x, axis=-1, keepdims=True) + eps) * g


def _inproj_kernel(x_ref, taba_ref, tabd_ref, w1_ref, qn_ref, wuq_ref, kvn_ref, wk_ref, wv_ref, bg_ref,
                   qmt_ref, km_ref, vmt_ref, qdt_ref, kd_ref, vdt_ref, us_ref, gates_ref,
                   *, q_scale_mla, q_scale_diff, d_model):
    xb = x_ref[...].astype(_BF16)
    n_mla = MLA_HEADS
    lat = _dot(xb, w1_ref[:, 0:4 * LANE])
    cqn = _rms(lat[:, :MLA_Q_LORA], qn_ref[...], RMS_EPS).astype(_BF16)
    q = _dot(cqn, wuq_ref[...])
    for h in range(n_mla):
        sl = slice(h * LANE, (h + 1) * LANE)
        qh = _rope_chunk(q[:, sl], taba_ref, MLA_ROPE // 2) * q_scale_mla
        qmt_ref[0, sl, :] = qh.T.astype(_BF16)
    ckvn = _rms(lat[:, MLA_Q_LORA:MLA_Q_LORA + MLA_KV_LORA], kvn_ref[...], RMS_EPS).astype(_BF16)
    kn = _dot(ckvn, wk_ref[...])
    kr = _rope_chunk(lat[:, 3 * LANE:4 * LANE], taba_ref, MLA_ROPE // 2)
    for h in range(n_mla):
        sl = slice(h * LANE, (h + 1) * LANE)
        km_ref[:, sl] = (kn[:, sl] + kr).astype(_BF16)
    vm = _dot(ckvn, wv_ref[...])
    seg = DIFF_HEADS * 2 * DIFF_HEAD_DIM
    for c in range(seg // LANE):
        sl = slice(c * LANE, (c + 1) * LANE)
        vmt_ref[0, sl, :] = vm[:, sl].T.astype(_BF16)

    base = 4 * LANE
    lane = lax.broadcasted_iota(jnp.int32, (1, LANE), 1)
    comp = lane // DIFF_HEAD_DIM
    qd = _dot(xb, w1_ref[:, base:base + seg])
    kd = _dot(xb, w1_ref[:, base + seg:base + 2 * seg])
    vd = _dot(xb, w1_ref[:, base + 2 * seg:base + 3 * seg])
    for c in range(seg // LANE):
        sl = slice(c * LANE, (c + 1) * LANE)
        qc = _rope_chunk(qd[:, sl], tabd_ref, DIFF_ROT // 2) * q_scale_diff
        qdt_ref[0, sl, :] = qc.T.astype(_BF16)
        kc = _rope_chunk(kd[:, sl], tabd_ref, DIFF_ROT // 2)
        for v in range(4):
            kd_ref[v, :, sl] = jnp.where(comp == v, kc, 0.0).astype(_BF16)
        vdt_ref[0, sl, :] = vd[:, sl].T.astype(_BF16)
    us = _dot(xb, w1_ref[:, base + 3 * seg:base + 4 * seg])
    for c in range(seg // LANE):
        us_ref[pl.ds(c, us.shape[0], stride=seg // LANE), :] = us[:, c * LANE:(c + 1) * LANE]
    gbase = base + 4 * seg
    for c in range(N_BRANCH):
        sl = slice(c * d_model, (c + 1) * d_model)
        g = _dot(xb, w1_ref[:, gbase + c * d_model:gbase + (c + 1) * d_model]) + bg_ref[:, sl]
        gates_ref[:, sl] = jax.nn.sigmoid(g).astype(_BF16)


def _inproj(x2d, taba, tabd, w1, qn, wuq, kvn, wk, wv, bg, tm):
    t, d_model = x2d.shape
    nt = t // tm
    seg = DIFF_HEADS * 2 * DIFF_HEAD_DIM
    hm = MLA_HEADS * LANE
    row = lambda w: pl.BlockSpec((tm, w), lambda i: (i, 0))
    colt = lambda w: pl.BlockSpec((1, w, tm), lambda i: (i, 0, 0))
    tab = pl.BlockSpec((3, tm, LANE), lambda i: (0, i, 0))
    kern = functools.partial(
        _inproj_kernel,
        q_scale_mla=float((MLA_NOPE + MLA_ROPE) ** -0.5 * LOG2E),
        q_scale_diff=float(DIFF_HEAD_DIM ** -0.5 * LOG2E),
        d_model=d_model)
    out_shapes = (
        jax.ShapeDtypeStruct((nt, hm, tm), _BF16), jax.ShapeDtypeStruct((t, hm), _BF16),
        jax.ShapeDtypeStruct((nt, MLA_HEADS * MLA_V, tm), _BF16),
        jax.ShapeDtypeStruct((nt, seg, tm), _BF16), jax.ShapeDtypeStruct((4, t, seg), _BF16),
        jax.ShapeDtypeStruct((nt, seg, tm), _BF16), jax.ShapeDtypeStruct((t * (seg // LANE), LANE), _F32),
        jax.ShapeDtypeStruct((t, N_BRANCH * d_model), _BF16))
    out_specs = (colt(hm), row(hm), colt(MLA_HEADS * MLA_V), colt(seg),
                 pl.BlockSpec((4, tm, seg), lambda i: (0, i, 0)), colt(seg),
                 pl.BlockSpec((tm * (seg // LANE), LANE), lambda i: (i, 0)), row(N_BRANCH * d_model))
    return pl.pallas_call(
        kern,
        grid=(nt,),
        in_specs=[row(d_model), tab, tab, _const_spec(w1.shape), _const_spec(qn.shape), _const_spec(wuq.shape),
                  _const_spec(kvn.shape), _const_spec(wk.shape), _const_spec(wv.shape), _const_spec(bg.shape)],
        out_specs=out_specs,
        out_shape=out_shapes,
        compiler_params=_cparams(1),
        name="inproj",
    )(x2d, taba, tabd, w1, qn, wuq, kvn, wk, wv, bg)


def _flash_maps(q_ts, load_ks, load_vt, qi, tq, s_scrs):
    n = len(q_ts)
    a_scrs = s_scrs[n:]

    def qk(i, j):
        s_scrs[i][...] = _dot(load_ks[i](j), q_ts[i])

    def step(j, carry, diagonal):
        vt = load_vt(j)
        if diagonal:
            r = lax.broadcasted_iota(jnp.int32, (tq, tq), 0)
            c = lax.broadcasted_iota(jnp.int32, (tq, tq), 1)
            keep = r <= c
        out = []
        for i in range(n):
            if i + 1 < n:
                qk(i + 1, j)
            elif not diagonal:
                qk(0, j + 1)
            m, l = carry[i]
            s = s_scrs[i][...]
            if diagonal:
                s = jnp.where(keep, s, NEG_BIG)
            m_new = jnp.maximum(m, jnp.max(s, axis=0, keepdims=True))
            alpha = jnp.exp2(m - m_new)
            p = jnp.exp2(s - m_new)
            l = alpha * l + jnp.sum(p, axis=0, keepdims=True)
            a_scrs[i][...] = alpha * a_scrs[i][...] + _dot(vt, p.astype(_BF16))
            out.append((m_new, l))
        return tuple(out)

    init = tuple((jnp.full((1, tq), NEG_BIG, _F32), jnp.zeros((1, tq), _F32)) for _ in range(n))
    for i in range(n):
        a_scrs[i][...] = jnp.zeros_like(a_scrs[i])
    qk(0, 0)
    carry = lax.fori_loop(0, qi, lambda j, c: step(j, c, False), init)
    carry = step(qi, carry, True)
    return [a_scrs[i][...] / carry[i][1] for i in range(n)]


def _key_block(k_ref, lead, j, tq, lanes):
    return k_ref[lead + (pl.ds(pl.multiple_of(j * tq, tq), tq), lanes)]


def _mla_attn_kernel(qt_ref, k_ref, vt_ref, o_ref, *s_scrs, tq):
    qi = pl.program_id(2)
    sls = [slice(hh * LANE, (hh + 1) * LANE) for hh in range(2)]
    outs = _flash_maps([qt_ref[0, sl, :] for sl in sls],
                       [lambda j, sl=sl: _key_block(k_ref, (0,), j, tq, sl) for sl in sls],
                       lambda j: vt_ref[0, j], qi, tq, s_scrs)
    ot = jnp.concatenate([outs[0][:MLA_V], outs[1][MLA_V:]], axis=0)
    o_ref[0] = ot.T.astype(o_ref.dtype)


def _mla_attention(qt, k, vt, b, tq):
    s = k.shape[1]
    nq = s // tq
    pairs = MLA_HEADS // 2
    vt4 = vt.reshape(b, nq, vt.shape[1], tq)
    return pl.pallas_call(
        functools.partial(_mla_attn_kernel, tq=tq),
        grid=(b, pairs, nq),
        in_specs=[pl.BlockSpec((1, 2 * LANE, tq), lambda bi, p, qi: (bi * nq + qi, p, 0)),
                  pl.BlockSpec((1, s, 2 * LANE), lambda bi, p, qi: (bi, 0, p)),
                  pl.BlockSpec((1, nq, LANE, tq), lambda bi, p, qi: (bi, 0, p, 0))],
        out_specs=pl.BlockSpec((1, tq, LANE), lambda bi, p, qi: (bi, qi, p)),
        out_shape=jax.ShapeDtypeStruct((b, s, pairs * LANE), _BF16),
        scratch_shapes=[pltpu.VMEM((tq, tq), _F32)] * 2 + [pltpu.VMEM((LANE, tq), _F32)] * 2,
        compiler_params=_cparams(3),
        name="mla_attention",
    )(qt, k, vt4)


def _diff_attn_kernel(qt_ref, k_ref, vt_ref, lam_ref, subln_ref, o_ref, *s_scrs, tq, lambda_init):
    qi = pl.program_id(2)
    qt = qt_ref[0]
    outs = _flash_maps([qt] * 4,
                       [lambda j, v=v: _key_block(k_ref, (v, 0), j, tq, slice(None)) for v in range(4)],
                       lambda j: vt_ref[0, j], qi, tq, s_scrs)
    lf = lam_ref[...]
    lam = (jnp.exp(jnp.sum(lf[0:1] * lf[1:2], axis=-1, keepdims=True))
           - jnp.exp(jnp.sum(lf[2:3] * lf[3:4], axis=-1, keepdims=True)) + lambda_init)
    half = 2 * DIFF_HEAD_DIM
    dt = jnp.concatenate([(outs[0] - lam * outs[1])[:half], (outs[2] - lam * outs[3])[half:]], axis=0)
    d = dt.T
    lane = lax.broadcasted_iota(jnp.int32, (tq, LANE), 1)
    first = lane < half
    sq = d * d
    ss_a = jnp.sum(jnp.where(first, sq, 0.0), axis=-1, keepdims=True)
    ss_b = jnp.sum(jnp.where(first, 0.0, sq), axis=-1, keepdims=True)
    ms = jnp.where(first, ss_a, ss_b) * (1.0 / half)
    o_ref[0] = (d * lax.rsqrt(ms + SUBLN_EPS) * subln_ref[...] * (1.0 - lambda_init)).astype(o_ref.dtype)


def _diff_attention(qt, kvar, vt, lam, subln2, b, tq, lambda_init):
    s = kvar.shape[2]
    nq = s // tq
    pairs = DIFF_HEADS // 2
    vt4 = vt.reshape(b, nq, vt.shape[1], tq)
    return pl.pallas_call(
        functools.partial(_diff_attn_kernel, tq=tq, lambda_init=lambda_init),
        grid=(b, pairs, nq),
        in_specs=[pl.BlockSpec((1, LANE, tq), lambda bi, p, qi: (bi * nq + qi, p, 0)),
                  pl.BlockSpec((4, 1, s, LANE), lambda bi, p, qi: (0, bi, 0, p)),
                  pl.BlockSpec((1, nq, LANE, tq), lambda bi, p, qi: (bi, 0, p, 0)),
                  _const_spec(lam.shape), _const_spec(subln2.shape)],
        out_specs=pl.BlockSpec((1, tq, LANE), lambda bi, p, qi: (bi, qi, p)),
        out_shape=jax.ShapeDtypeStruct((b, s, pairs * LANE), _BF16),
        scratch_shapes=[pltpu.VMEM((tq, tq), _F32)] * 4 + [pltpu.VMEM((LANE, tq), _F32)] * 4,
        compiler_params=_cparams(3),
        name="diff_attention",
    )(qt, kvar, vt4, lam, subln2)


def _ssm_pack_kernel(x_ref, o_ref, *, ct, lane_tiles):
    L, P = SSM_CHUNK, SSM_GROUP
    per_lane_tile = LANE // P
    for s_in in range(L):
        for lt in range(lane_tiles):
            xs = x_ref[pl.ds(s_in * lane_tiles + lt, ct, stride=L * lane_tiles), :]
            xt = xs.T
            for gi in range(per_lane_tile):
                o_ref[lt * per_lane_tile + gi, s_in * P:(s_in + 1) * P, :] = (
                    xt[gi * P:(gi + 1) * P, :].astype(o_ref.dtype))


def _ssm_unpack_kernel(y_ref, o_ref, *, ct, lane_tiles):
    L, P = SSM_CHUNK, SSM_GROUP
    per_lane_tile = LANE // P
    for t_in in range(L):
        for lt in range(lane_tiles):
            z = jnp.concatenate([y_ref[lt * per_lane_tile + gi, t_in * P:(t_in + 1) * P, :].astype(_F32)
                                 for gi in range(per_lane_tile)], axis=0)
            o_ref[pl.ds(t_in * lane_tiles + lt, ct, stride=L * lane_tiles), :] = z.T


def _ssm_tiles(t, batch):
    chunks = t // batch // SSM_CHUNK
    ct = min(LANE, chunks)
    assert chunks % ct == 0
    return chunks, ct


def _ssm_pack(us, batch, groups):
    lane_tiles = groups * SSM_GROUP // LANE
    t = us.shape[0] // lane_tiles
    chunks, ct = _ssm_tiles(t, batch)
    return pl.pallas_call(
        functools.partial(_ssm_pack_kernel, ct=ct, lane_tiles=lane_tiles),
        grid=(t // (ct * SSM_CHUNK),),
        in_specs=[pl.BlockSpec((ct * SSM_CHUNK * lane_tiles, LANE), lambda i: (i, 0))],
        out_specs=pl.BlockSpec((groups, SSM_CHUNK * SSM_GROUP, ct), lambda i: (0, 0, i)),
        out_shape=jax.ShapeDtypeStruct((groups, SSM_CHUNK * SSM_GROUP, batch * chunks), _BF16),
        compiler_params=_cparams(1),
        name="ssm_pack",
    )(us)


def _ssm_unpack(yt, batch):
    groups, w, r = yt.shape
    t = r * SSM_CHUNK
    chunks, ct = _ssm_tiles(t, batch)
    lane_tiles = groups * SSM_GROUP // LANE
    return pl.pallas_call(
        functools.partial(_ssm_unpack_kernel, ct=ct, lane_tiles=lane_tiles),
        grid=(r // ct,),
        in_specs=[pl.BlockSpec((groups, w, ct), lambda i: (0, 0, i))],
        out_specs=pl.BlockSpec((ct * SSM_CHUNK * lane_tiles, LANE), lambda i: (i, 0)),
        out_shape=jax.ShapeDtypeStruct((t * lane_tiles, LANE), _F32),
        compiler_params=_cparams(1),
        name="ssm_unpack",
    )(yt)


def _ssm_kernel(u_ref, t_ref, b_ref, c_ref, ar_ref, ai_ref, d_ref, y_ref, x_scr, xs_scr, h_scr, *, batch, chunks):
    ut = u_ref[0]
    xt = _dot(b_ref[0], ut)
    x_scr[...] = xt.T
    xs_scr[...] = jnp.concatenate([xt[SSM_STATE:], xt[:SSM_STATE]], axis=0).T
    ar, ai = ar_ref[0], ai_ref[0]
    ai_s = pltpu.roll(ai, SSM_STATE, 1)

    def step(c, carry):
        h, hs = carry
        rows = pl.ds(c, batch, stride=chunks)
        h_scr[rows, :] = h
        return ar * h + ai * hs + x_scr[rows, :], ar * hs + ai_s * h + xs_scr[rows, :]

    zero = jnp.zeros((batch, 2 * SSM_STATE), _F32)
    lax.fori_loop(0, chunks, step, (zero, zero), unroll=SCAN_UNROLL)
    ht = h_scr[...].T.astype(_BF16)
    yt = _dot(t_ref[0], ut) + _dot(c_ref[0], ht) + ut.astype(_F32) * d_ref[0]
    y_ref[0] = jax.nn.gelu(yt).astype(y_ref.dtype)


def _ssm(u_t, tmat_t, bmat_t, cmat_t, ar, ai, d_col, batch):
    g, w, r = u_t.shape
    per_g = lambda shape: pl.BlockSpec((1,) + shape, lambda i: (i, 0, 0))
    return pl.pallas_call(
        functools.partial(_ssm_kernel, batch=batch, chunks=r // batch),
        grid=(g,),
        in_specs=[per_g((w, r)), per_g((w, w)), per_g((2 * SSM_STATE, w)), per_g((w, 2 * SSM_STATE)),
                  per_g((1, 2 * SSM_STATE)), per_g((1, 2 * SSM_STATE)), per_g((w, 1))],
        out_specs=per_g((w, r)),
        out_shape=jax.ShapeDtypeStruct((g, w, r), _BF16),
        scratch_shapes=[pltpu.VMEM((r, 2 * SSM_STATE), _F32)] * 3,
        compiler_params=_cparams(1),
        name="ssm_scan",
    )(u_t, tmat_t, bmat_t, cmat_t, ar, ai, d_col)


def _ssm_matrices(a_re, a_im, log_dt, b_re, b_im, c_re, c_im, d):
    L, P, N = SSM_CHUNK, SSM_GROUP, SSM_STATE
    G = a_re.shape[0]
    A = lax.complex(a_re.astype(_F32), a_im.astype(_F32))
    dt = jnp.exp(log_dt.astype(_F32))[:, None]
    a_bar = jnp.exp(A * dt)
    b_bar = ((a_bar - 1.0) / A)[..., None] * lax.complex(b_re.astype(_F32), b_im.astype(_F32))
    C = lax.complex(c_re.astype(_F32), c_im.astype(_F32))
    steps = jnp.arange(L + 1, dtype=_F32)
    apow = jnp.exp((A * dt)[None] * steps[:, None, None])
    m = jnp.real(jnp.einsum('gpn,dgn,gnq->dgpq', C, apow[:L], b_bar))
    s_idx = jnp.arange(L)[:, None]
    t_idx = jnp.arange(L)[None, :]
    lag = jnp.clip(t_idx - s_idx, 0, L - 1)
    tfull = jnp.where((t_idx >= s_idx)[:, :, None, None, None], m[lag], 0.0)
    tmat = tfull.transpose(2, 0, 4, 1, 3).reshape(G, L * P, L * P)
    bm = apow[L - 1 - jnp.arange(L)][:, :, :, None] * b_bar[None]
    bm = bm.transpose(1, 0, 3, 2).reshape(G, L * P, N)
    bmat = jnp.concatenate([jnp.real(bm), jnp.imag(bm)], axis=-1)
    cm = C[None] * apow[1:L + 1][:, :, None, :]
    cm = cm.transpose(1, 3, 0, 2).reshape(G, N, L * P)
    cmat = jnp.concatenate([jnp.real(cm), -jnp.imag(cm)], axis=1)
    al = apow[L]
    ar = jnp.concatenate([jnp.real(al), jnp.real(al)], axis=-1)[:, None, :]
    ai = jnp.concatenate([-jnp.imag(al), jnp.imag(al)], axis=-1)[:, None, :]
    dflat = jnp.tile(d.astype(_F32).reshape(G, 1, P), (1, L, 1)).reshape(G, 1, L * P)
    tr = lambda a: a.transpose(0, 2, 1).astype(_BF16)
    return tr(tmat), tr(bmat), tr(cmat), ar, ai, dflat.reshape(G, L * P, 1)


def _layernorm(z, g, b):
    mu = jnp.mean(z, axis=-1, keepdims=True)
    zc = z - mu
    var = jnp.mean(zc * zc, axis=-1, keepdims=True)
    return zc * lax.rsqrt(var + LN_EPS) * g + b


def _merge_kernel(om_ref, od_ref, ys_ref, g_ref, x_ref, wmo_ref, wdo_ref, wglu_ref, wo_ref, lng_ref, lnb_ref,
                  rw_ref, rb_ref, x1_ref, idx_ref, wts_ref, cnt_ref, *, alpha, d_model):
    def branch_dots(rows):
        n_rows = rows.stop - rows.start
        lt = wglu_ref.shape[0] // LANE
        ys = jnp.concatenate([ys_ref[pl.ds(rows.start * lt + c, n_rows, stride=lt), :] for c in range(lt)], axis=1)
        return (_dot(om_ref[rows, :], wmo_ref[...]), _dot(od_ref[rows, :], wdo_ref[...]),
                _dot(ys.astype(_BF16), wglu_ref[...]))

    def gate_merge(rows, ya, yb, gl):
        yc = gl[:, :d_model] * jax.nn.sigmoid(gl[:, d_model:])
        merged = (g_ref[rows, 0:d_model].astype(_F32) * ya + g_ref[rows, d_model:2 * d_model].astype(_F32) * yb
                  + g_ref[rows, 2 * d_model:3 * d_model].astype(_F32) * yc)
        return merged.astype(_BF16)

    def out_norm(rows, merged):
        z = alpha * x_ref[rows, :] + _dot(merged, wo_ref[...])
        x1 = _layernorm(z, lng_ref[...], lnb_ref[...])
        for c in range(d_model // LANE):
            x1_ref[pl.ds(rows.start * TOKEN_TILE + c, rows.stop - rows.start, stride=TOKEN_TILE), :] = (
                x1[:, c * LANE:(c + 1) * LANE])
        return x1

    def router_logits(x1):
        return _dot_nt(rw_ref[...], x1, precision=lax.Precision.HIGHEST)

    def route(rows, logits):
        scores = jax.nn.sigmoid(logits)
        biased = scores + rb_ref[...]
        epg = logits.shape[0] // N_GROUPS
        a = [biased[j * N_GROUPS:(j + 1) * N_GROUPS] for j in range(epg)]
        sc = [scores[j * N_GROUPS:(j + 1) * N_GROUPS] for j in range(epg)]
        gs = None
        for i in range(epg):
            for j in range(i + 1, epg):
                pair = a[i] + a[j]
                gs = pair if gs is None else jnp.maximum(gs, pair)
        giota = lax.broadcasted_iota(jnp.int32, gs.shape, 0)
        gmax = jnp.max(gs, axis=0, keepdims=True)
        gsel = jnp.min(jnp.where(gs == gmax, giota, N_GROUPS), axis=0, keepdims=True)
        hot = giota == gsel
        val = [jnp.sum(jnp.where(hot, a[j], 0.0), axis=0, keepdims=True) for j in range(epg)]
        raw = [jnp.sum(jnp.where(hot, sc[j], 0.0), axis=0, keepdims=True) for j in range(epg)]
        b1, i1, w1 = val[0], jnp.zeros_like(gsel), raw[0]
        for j in range(1, epg):
            take = val[j] > b1
            b1 = jnp.where(take, val[j], b1)
            i1 = jnp.where(take, j, i1)
            w1 = jnp.where(take, raw[j], w1)
        b2 = jnp.full_like(b1, -jnp.inf)
        i2 = jnp.zeros_like(gsel)
        w2 = jnp.zeros_like(w1)
        for j in range(epg):
            take = (i1 != j) & (val[j] > b2)
            b2 = jnp.where(take, val[j], b2)
            i2 = jnp.where(take, j, i2)
            w2 = jnp.where(take, raw[j], w2)
        wsum = w1 + w2
        e1, e2 = gsel * epg + i1, gsel * epg + i2
        idx_ref[:, rows] = jnp.concatenate([e1, e2], axis=0)
        wts_ref[:, rows] = jnp.concatenate([w1 / wsum, w2 / wsum], axis=0)
        eiota = lax.broadcasted_iota(jnp.int32, logits.shape, 0)
        hit = (eiota == e1).astype(jnp.int32) + (eiota == e2).astype(jnp.int32)
        part = hit[:, 0:LANE]
        for c in range(1, hit.shape[1] // LANE):
            part = part + hit[:, c * LANE:(c + 1) * LANE]
        return part

    tm = x_ref.shape[0]
    sub = tm // MERGE_SUBTILES
    rows = [slice(k * sub, (k + 1) * sub) for k in range(MERGE_SUBTILES)]
    dots = [branch_dots(r) for r in rows]
    merged = [gate_merge(r, *d) for r, d in zip(rows, dots)]
    x1s = [out_norm(r, m) for r, m in zip(rows, merged)]
    logits = [router_logits(x1) for x1 in x1s]
    parts = [route(r, lg) for r, lg in zip(rows, logits)]

    @pl.when(pl.program_id(0) == 0)
    def _():
        cnt_ref[...] = jnp.zeros_like(cnt_ref)

    cnt_ref[...] += functools.reduce(lambda p, q: p + q, parts)


def _merge(om, od, ys, gates, x2d, wmo, wdo, wglu, wo, lng, lnb, rw, rb, tm, alpha):
    t, d_model = x2d.shape
    n_experts = rw.shape[0]
    assert d_model == TOKEN_TILE * LANE and tm % MERGE_SUBTILES == 0
    row = lambda w: pl.BlockSpec((tm, w), lambda i: (i, 0))
    col = pl.BlockSpec((TOP_K, tm), lambda i: (0, i))
    return pl.pallas_call(
        functools.partial(_merge_kernel, alpha=alpha, d_model=d_model),
        grid=(t // tm,),
        in_specs=[row(om.shape[1]), row(od.shape[1]),
                  pl.BlockSpec((tm * (ys.shape[0] // t), LANE), lambda i: (i, 0)), row(gates.shape[1]), row(d_model),
                  _const_spec(wmo.shape), _const_spec(wdo.shape), _const_spec(wglu.shape), _const_spec(wo.shape),
                  _const_spec(lng.shape), _const_spec(lnb.shape), _const_spec(rw.shape), _const_spec(rb.shape)],
        out_specs=(pl.BlockSpec((tm * TOKEN_TILE, LANE), lambda i: (i, 0)), col, col,
                   pl.BlockSpec((n_experts, LANE), lambda i: (0, 0))),
        out_shape=(jax.ShapeDtypeStruct((t * TOKEN_TILE, LANE), _F32), jax.ShapeDtypeStruct((TOP_K, t), jnp.int32),
                   jax.ShapeDtypeStruct((TOP_K, t), _F32), jax.ShapeDtypeStruct((n_experts, LANE), jnp.int32)),
        compiler_params=_cparams(1),
        name="merge_router",
    )(om, od, ys, gates, x2d, wmo, wdo, wglu, wo, lng, lnb, rw, rb)


DMA_UNROLL = 8


def _moe_kernel(be_ref, cnt_ref, nu_ref, tok_ref, tokn_ref, dst_ref, x_hbm, wg_ref, wu_ref, wd_ref, out_hbm,
                xbuf, ybuf, wgb, wub, wdb, sem_in, sem_out, *, mb):
    i = pl.program_id(0)
    n_used = nu_ref[0]
    slot = lax.rem(i, 2)

    def for_rows(start_row):
        def body(j, carry):
            for u in range(DMA_UNROLL):
                start_row(j * DMA_UNROLL + u, u % 2)
            return carry
        lax.fori_loop(0, mb // DMA_UNROLL, body, 0)

    def for_first_rows(n, fn):
        def body(r, carry):
            fn(r)
            return carry
        lax.fori_loop(0, n, body, 0)

    def tile_rows(tok):
        return pl.ds(pl.multiple_of(tok * TOKEN_TILE, TOKEN_TILE), TOKEN_TILE)

    def gather_row(t_ref, s, r):
        return pltpu.make_async_copy(x_hbm.at[tile_rows(t_ref[0, 0, r]), :], xbuf.at[s, tile_rows(r), :],
                                     sem_in.at[s])

    def scatter_row(s, r, dst):
        return pltpu.make_async_copy(ybuf.at[s, tile_rows(r), :], out_hbm.at[tile_rows(dst), :], sem_out.at[s])

    def issue_gather(t_ref, s):
        for_rows(lambda r, pr: gather_row(t_ref, s, r).start(priority=pr))

    def wait_gather(s):
        pltpu.make_async_copy(x_hbm.at[pl.ds(0, mb * TOKEN_TILE), :], xbuf.at[s], sem_in.at[s]).wait()

    def issue_scatter(s, cnt):
        @pl.when(cnt == mb)
        def _():
            for_rows(lambda r, pr: scatter_row(s, r, dst_ref[0, 0, r]).start(priority=pr))

        @pl.when(cnt < mb)
        def _():
            for_first_rows(cnt, lambda r: scatter_row(s, r, dst_ref[0, 0, r]).start())

    def wait_scatter(s, cnt):
        @pl.when(cnt == mb)
        def _():
            pltpu.make_async_copy(ybuf.at[s], out_hbm.at[pl.ds(0, mb * TOKEN_TILE), :], sem_out.at[s]).wait()

        @pl.when(cnt < mb)
        def _():
            for_first_rows(cnt, lambda r: scatter_row(s, r, 0).wait())

    @pl.when(i < n_used)
    def _():
        @pl.when(i == 0)
        def _():
            issue_gather(tok_ref, 0)

        @pl.when(i + 1 < n_used)
        def _():
            issue_gather(tokn_ref, 1 - slot)

        @pl.when((i == 0) | (be_ref[i] != be_ref[jnp.maximum(i - 1, 0)]))
        def _():
            wgb[...] = wg_ref[0, 0].astype(_BF16)
            wub[...] = wu_ref[0, 0].astype(_BF16)
            wdb[...] = wd_ref[0, 0].astype(_BF16)

        wait_gather(slot)

        @pl.when(i >= 2)
        def _():
            wait_scatter(slot, cnt_ref[jnp.maximum(i - 2, 0)])

        xb = jnp.concatenate([xbuf[slot, pl.ds(c, mb, stride=TOKEN_TILE), :] for c in range(TOKEN_TILE)],
                             axis=1).astype(_BF16)
        h = jax.nn.silu(_dot(xb, wgb[...])) * _dot(xb, wub[...])
        y = _dot(h.astype(_BF16), wdb[...])
        for c in range(TOKEN_TILE):
            ybuf[slot, pl.ds(c, mb, stride=TOKEN_TILE), :] = y[:, c * LANE:(c + 1) * LANE]
        issue_scatter(slot, cnt_ref[i])

        @pl.when(i == n_used - 1)
        def _():
            @pl.when(i >= 1)
            def _():
                wait_scatter(1 - slot, cnt_ref[jnp.maximum(i - 1, 0)])
            wait_scatter(slot, cnt_ref[i])


def _moe(block_e, block_cnt, n_used, row_tok, row_dst, x1t, wg, wu, wd, layer):
    t = x1t.shape[0] // TOKEN_TILE
    d_model = TOKEN_TILE * LANE
    n_blocks, mb = row_tok.shape
    ff = wg.shape[-1]
    assert mb % DMA_UNROLL == 0
    smem_row = lambda f: pl.BlockSpec((1, 1, mb), lambda i, be, cnt, nu: (f(i), 0, 0), memory_space=pltpu.SMEM)
    wspec = lambda a, b: pl.BlockSpec((1, 1, a, b), lambda i, be, cnt, nu: (layer, be[i], 0, 0))
    grid_spec = pltpu.PrefetchScalarGridSpec(
        num_scalar_prefetch=3,
        grid=(n_blocks,),
        in_specs=[smem_row(lambda i: i), smem_row(lambda i: jnp.minimum(i + 1, n_blocks - 1)), smem_row(lambda i: i),
                  pl.BlockSpec(memory_space=pl.ANY),
                  wspec(d_model, ff), wspec(d_model, ff), wspec(ff, d_model)],
        out_specs=pl.BlockSpec(memory_space=pl.ANY),
        scratch_shapes=[pltpu.VMEM((2, mb * TOKEN_TILE, LANE), _F32), pltpu.VMEM((2, mb * TOKEN_TILE, LANE), _F32),
                        pltpu.VMEM((d_model, ff), _BF16), pltpu.VMEM((d_model, ff), _BF16),
                        pltpu.VMEM((ff, d_model), _BF16),
                        pltpu.SemaphoreType.DMA((2,)), pltpu.SemaphoreType.DMA((2,))])
    tok3 = row_tok.reshape(n_blocks, 1, mb)
    return pl.pallas_call(
        functools.partial(_moe_kernel, mb=mb),
        grid_spec=grid_spec,
        out_shape=jax.ShapeDtypeStruct((TOP_K * t * TOKEN_TILE, LANE), _F32),
        compiler_params=_cparams(1),
        name="moe_experts",
    )(block_e, block_cnt, n_used, tok3, tok3, row_dst.reshape(n_blocks, 1, mb), x1t, wg, wu, wd)


def _moe_plan(idx, counts, mb):
    k, t = idx.shape
    a_total = k * t
    n_experts = counts.shape[0]
    e_flat = idx.reshape(a_total)
    order = jnp.argsort(e_flat).astype(jnp.int32)
    starts = jnp.cumsum(counts) - counts
    padded = (counts + mb - 1) // mb * mb
    pad_ends = jnp.cumsum(padded)
    pad_starts = pad_ends - padded
    n_blocks = -(-a_total // mb) + n_experts
    blk = jnp.arange(n_blocks, dtype=jnp.int32)
    blk_start = blk * mb
    block_e = jnp.minimum(jnp.sum((pad_ends[None, :] <= blk_start[:, None]).astype(jnp.int32), axis=1),
                          n_experts - 1)
    off0 = blk_start - pad_starts[block_e]
    cnt = jnp.clip(counts[block_e] - off0, 0, mb)
    n_used = jnp.sum((cnt > 0).astype(jnp.int32)).reshape(1)
    lane = jnp.arange(mb, dtype=jnp.int32)[None, :]
    valid = lane < cnt[:, None]
    src = jnp.clip((starts[block_e] + off0)[:, None] + lane, 0, a_total - 1)
    a_r = order[src]
    row_dst = jnp.where(valid, a_r, 0).astype(jnp.int32)
    row_tok = jnp.where(valid, a_r % t, 0).astype(jnp.int32)
    return block_e.astype(jnp.int32), cnt.astype(jnp.int32), n_used, row_tok, row_dst


def _final_kernel(x_ref, y0_ref, y1_ref, w_ref, lng_ref, lnb_ref, o_ref, *, alpha):
    tm = o_ref.shape[0]

    def tokens(ref):
        return jnp.concatenate([ref[pl.ds(c, tm, stride=TOKEN_TILE), :] for c in range(TOKEN_TILE)], axis=1)

    w = w_ref[...]
    z = alpha * tokens(x_ref) + (tokens(y0_ref) * w[:, 0:1] + tokens(y1_ref) * w[:, 1:2])
    o_ref[...] = _layernorm(z, lng_ref[...], lnb_ref[...])


def _final(x1t, yslots, wts_t, lng, lnb, tm, alpha):
    t = x1t.shape[0] // TOKEN_TILE
    d_model = TOKEN_TILE * LANE
    nt = t // tm
    tile = lambda f: pl.BlockSpec((tm * TOKEN_TILE, LANE), lambda i: (f(i), 0))
    return pl.pallas_call(
        functools.partial(_final_kernel, alpha=alpha),
        grid=(nt,),
        in_specs=[tile(lambda i: i), tile(lambda i: i), tile(lambda i: nt + i),
                  pl.BlockSpec((tm, TOP_K), lambda i: (i, 0)),
                  _const_spec(lng.shape), _const_spec(lnb.shape)],
        out_specs=pl.BlockSpec((tm, d_model), lambda i: (i, 0)),
        out_shape=jax.ShapeDtypeStruct((t, d_model), _F32),
        compiler_params=_cparams(1),
        name="moe_residual_ln",
    )(x1t, yslots, yslots, wts_t, lng, lnb)


def _prep_w_in(w):
    d_model = w.shape[0]
    seg = DIFF_HEADS * 2 * DIFF_HEAD_DIM
    o = 0
    cq = w[:, o:o + MLA_Q_LORA]; o += MLA_Q_LORA
    ckv = w[:, o:o + MLA_KV_LORA]; o += MLA_KV_LORA
    kr = w[:, o:o + MLA_ROPE]; o += MLA_ROPE
    rest = w[:, o:]
    z = lambda n: jnp.zeros((d_model, n), w.dtype)
    kr_chunk = jnp.concatenate([z(MLA_NOPE), kr, z(LANE - MLA_NOPE - MLA_ROPE)], axis=1)
    assert MLA_Q_LORA + MLA_KV_LORA + LANE == 4 * LANE and rest.shape[1] == 4 * seg + N_BRANCH * d_model
    return jnp.concatenate([cq, ckv, kr_chunk, rest], axis=1).astype(_BF16)


def _prep_w_uq(w):
    dq = MLA_NOPE + MLA_ROPE
    w3 = w.reshape(w.shape[0], MLA_HEADS, dq)
    w3 = jnp.pad(w3, ((0, 0), (0, 0), (0, LANE - dq)))
    return w3.reshape(w.shape[0], MLA_HEADS * LANE).astype(_BF16)


def _prep_w_ukv(w):
    w3 = w.reshape(w.shape[0], MLA_HEADS, MLA_NOPE + MLA_V)
    wk = jnp.pad(w3[:, :, :MLA_NOPE], ((0, 0), (0, 0), (0, LANE - MLA_NOPE)))
    wv = w3[:, :, MLA_NOPE:]
    return (wk.reshape(w.shape[0], MLA_HEADS * LANE).astype(_BF16),
            wv.reshape(w.shape[0], MLA_HEADS * MLA_V).astype(_BF16))


def kernel(x, positions, w_in, b_gate, mla_q_norm, mla_w_uq, mla_kv_norm, mla_w_ukv, mla_w_out,
           diff_lambda, diff_subln, diff_w_out, ssm_a_re, ssm_a_im, ssm_log_dt, ssm_b_re, ssm_b_im,
           ssm_c_re, ssm_c_im, ssm_d, ssm_w_glu, w_o, ln_gain, ln_bias, router_w, router_bias,
           moe_w_gate, moe_w_up, moe_w_down):
    B, S, D = x.shape
    T = B * S
    depth = w_in.shape[0]
    n_experts = router_w.shape[1]
    epg = n_experts // N_GROUPS
    alpha = float((2 * depth) ** 0.25)
    tm = min(ROW_TILE, T)
    tq = min(ATTN_TILE, S)
    L, P = SSM_CHUNK, SSM_GROUP
    G = ssm_a_re.shape[1]
    assert T % tm == 0 and S % tq == 0 and S % L == 0 and (TOP_K * T) % MOE_ROWS == 0
    assert MLA_HEADS * MLA_V == DIFF_HEADS * 2 * DIFF_HEAD_DIM == G * P

    (inv_a, sel_a), (inv_d, sel_d) = _rope_lane_patterns()
    pos_col = positions.reshape(T, 1).astype(jnp.int32)
    taba = _rope_tables(pos_col, inv_a, sel_a, tm)
    tabd = _rope_tables(pos_col, inv_d, sel_d, tm)

    rw = router_w.T.reshape(N_GROUPS, epg, D).transpose(1, 0, 2).reshape(n_experts, D).astype(_F32)
    rb = router_bias.reshape(N_GROUPS, epg).T.reshape(n_experts, 1).astype(_F32)

    x2d = x.reshape(T, D)
    for l in range(depth):
        lambda_init = 0.8 - 0.6 * math.exp(-0.3 * l)
        w1 = _prep_w_in(w_in[l])
        wuq = _prep_w_uq(mla_w_uq[l])
        wk, wv = _prep_w_ukv(mla_w_ukv[l])
        qmt, km, vmt, qdt, kdv, vdt, us, gates = _inproj(
            x2d, taba, tabd, w1, mla_q_norm[l][None].astype(_F32), wuq, mla_kv_norm[l][None].astype(_F32),
            wk, wv, b_gate[l].reshape(1, N_BRANCH * D).astype(_F32), tq)

        o_mla = _mla_attention(qmt, km.reshape(B, S, -1), vmt, B, tq)
        subln2 = jnp.tile(diff_subln[l].astype(_F32), 2)[None]
        o_diff = _diff_attention(qdt, kdv.reshape(4, B, S, -1), vdt, diff_lambda[l].astype(_F32), subln2,
                                 B, tq, lambda_init)

        mats = _ssm_matrices(ssm_a_re[l], ssm_a_im[l], ssm_log_dt[l], ssm_b_re[l], ssm_b_im[l],
                             ssm_c_re[l], ssm_c_im[l], ssm_d[l])
        ys = _ssm_unpack(_ssm(_ssm_pack(us, B, G), *mats, batch=B), B)

        x1, idx, wts, cnt_part = _merge(
            o_mla.reshape(T, -1), o_diff.reshape(T, -1), ys, gates, x2d,
            mla_w_out[l].astype(_BF16), diff_w_out[l].astype(_BF16), ssm_w_glu[l].astype(_BF16),
            w_o[l].astype(_BF16), ln_gain[l, 0][None].astype(_F32), ln_bias[l, 0][None].astype(_F32),
            rw, rb, min(MERGE_TILE, T), alpha)

        plan = _moe_plan(idx, jnp.sum(cnt_part, axis=1), MOE_ROWS)
        yslots = _moe(*plan, x1, moe_w_gate.astype(_F32), moe_w_up.astype(_F32), moe_w_down.astype(_F32), l)
        x2d = _final(x1, yslots, wts.T, ln_gain[l, 1][None].astype(_F32), ln_bias[l, 1][None].astype(_F32),
                     tm, alpha)
    return x2d.reshape(B, S, D)
```

```python
import functools
import math

import jax
import jax.numpy as jnp
from jax import lax
from jax.experimental import pallas as pl
from jax.experimental.pallas import tpu as pltpu

MLA_HEADS = 8
MLA_Q_LORA = 256
MLA_KV_LORA = 128
MLA_NOPE = 64
MLA_ROPE = 32
MLA_V = 64
DIFF_HEADS = 8
DIFF_HEAD_DIM = 32
DIFF_ROT = DIFF_HEAD_DIM // 4
SSM_GROUP = 16
SSM_STATE = 64
N_BRANCH = 3
ROPE_THETA = 500000.0
N_GROUPS = 8
TOP_K = 2
LN_EPS = 1e-5
RMS_EPS = 1e-6
SUBLN_EPS = 1e-5

LANE = 128
ROW_TILE = 256
ATTN_TILE = 512
ATTN_PAIRS_PER_STEP = 2
SSM_CHUNK = 16
SCAN_UNROLL = 4
MOE_ROWS = 256
MERGE_TILE = 512
MERGE_SUBTILES = 2
TOKEN_TILE = 8
VMEM_LIMIT = 56 * 1024 * 1024

NEG_BIG = -1e30
LOG2E = math.log2(math.e)

_F32 = jnp.float32
_BF16 = jnp.bfloat16


def _cparams(n_axes):
    return pltpu.CompilerParams(dimension_semantics=("arbitrary",) * n_axes,
                                vmem_limit_bytes=VMEM_LIMIT)


def _dot(a, b):
    return jnp.dot(a, b, preferred_element_type=_F32)


def _dot_nt(a, b, precision=None):
    return lax.dot_general(a, b, (((1,), (1,)), ((), ())), precision=precision,
                           preferred_element_type=_F32)


def _const_spec(shape):
    zeros = (0,) * len(shape)
    return pl.BlockSpec(shape, lambda *_: zeros, pipeline_mode=pl.Buffered(1))


def _rope_table_kernel(pos_ref, inv_ref, sel_ref, out_ref):
    pos = pos_ref[...].astype(_F32)
    ang = pos * inv_ref[...]
    c, s = jnp.cos(ang), jnp.sin(ang)
    sel = sel_ref[...]
    out_ref[0] = c * (sel[0:1] + sel[1:2]) + sel[2:3]
    out_ref[1] = -s * sel[0:1]
    out_ref[2] = s * sel[1:2]


def _rope_tables(pos_col, inv_row, sel_rows, tm):
    t = pos_col.shape[0]
    return pl.pallas_call(
        _rope_table_kernel,
        grid=(t // tm,),
        in_specs=[pl.BlockSpec((tm, 1), lambda i: (i, 0)), _const_spec((1, LANE)), _const_spec((3, LANE))],
        out_specs=pl.BlockSpec((3, tm, LANE), lambda i: (0, i, 0)),
        out_shape=jax.ShapeDtypeStruct((3, t, LANE), _F32),
        compiler_params=_cparams(1),
        name="rope_tables",
    )(pos_col, inv_row, sel_rows)


def _rope_lane_patterns():
    lane = jnp.arange(LANE)
    half_a = MLA_ROPE // 2
    in_rope = (lane >= MLA_NOPE) & (lane < MLA_NOPE + MLA_ROPE)
    fi = (lane - MLA_NOPE) % half_a
    inv_a = jnp.where(in_rope, ROPE_THETA ** (-(2.0 * fi.astype(_F32)) / MLA_ROPE), 0.0)
    x1_a = in_rope & (lane < MLA_NOPE + half_a)
    x2_a = in_rope & ~x1_a
    sel_a = jnp.stack([x1_a, x2_a, ~in_rope]).astype(_F32)
    half_d = DIFF_ROT // 2
    d = lane % DIFF_HEAD_DIM
    in_rot = d < DIFF_ROT
    inv_d = jnp.where(in_rot, ROPE_THETA ** (-(2.0 * (d % half_d).astype(_F32)) / DIFF_ROT), 0.0)
    x1_d = d < half_d
    x2_d = in_rot & ~x1_d
    sel_d = jnp.stack([x1_d, x2_d, ~in_rot]).astype(_F32)
    return (inv_a[None].astype(_F32), sel_a), (inv_d[None].astype(_F32), sel_d)


def _rope_chunk(x, tab_ref, half):
    return (x * tab_ref[0] + pltpu.roll(x, LANE - half, 1) * tab_ref[1]
            + pltpu.roll(x, half, 1) * tab_ref[2])


def _rms(x, g, eps):
    return x * lax.rsqrt(jnp.mean(x * x, axis=-1, keepdims=True) + eps) * g


def _inproj_kernel(x_ref, taba_ref, tabd_ref, w1_ref, qn_ref, wuq_ref, kvn_ref, wk_ref, wv_ref, bg_ref,
                   qmt_ref, km_ref, vmt_ref, qdt_ref, kd_ref, vdt_ref, us_ref, gates_ref,
                   *, q_scale_mla, q_scale_diff, d_model):
    xb = x_ref[...].astype(_BF16)
    n_mla = MLA_HEADS
    lat = _dot(xb, w1_ref[:, 0:4 * LANE])
    cqn = _rms(lat[:, :MLA_Q_LORA], qn_ref[...], RMS_EPS).astype(_BF16)
    q = _dot(cqn, wuq_ref[...])
    for h in range(n_mla):
        sl = slice(h * LANE, (h + 1) * LANE)
        qh = _rope_chunk(q[:, sl], taba_ref, MLA_ROPE // 2) * q_scale_mla
        qmt_ref[0, sl, :] = qh.T.astype(_BF16)
    ckvn = _rms(lat[:, MLA_Q_LORA:MLA_Q_LORA + MLA_KV_LORA], kvn_ref[...], RMS_EPS).astype(_BF16)
    kn = _dot(ckvn, wk_ref[...])
    kr = _rope_chunk(lat[:, 3 * LANE:4 * LANE], taba_ref, MLA_ROPE // 2)
    for h in range(n_mla):
        sl = slice(h * LANE, (h + 1) * LANE)
        km_ref[:, sl] = (kn[:, sl] + kr).astype(_BF16)
    vm = _dot(ckvn, wv_ref[...])
    seg = DIFF_HEADS * 2 * DIFF_HEAD_DIM
    for c in range(seg // LANE):
        sl = slice(c * LANE, (c + 1) * LANE)
        vmt_ref[0, sl, :] = vm[:, sl].T.astype(_BF16)

    base = 4 * LANE
    lane = lax.broadcasted_iota(jnp.int32, (1, LANE), 1)
    comp = lane // DIFF_HEAD_DIM
    qd = _dot(xb, w1_ref[:, base:base + seg])
    kd = _dot(xb, w1_ref[:, base + seg:base + 2 * seg])
    vd = _dot(xb, w1_ref[:, base + 2 * seg:base + 3 * seg])
    for c in range(seg // LANE):
        sl = slice(c * LANE, (c + 1) * LANE)
        qc = _rope_chunk(qd[:, sl], tabd_ref, DIFF_ROT // 2) * q_scale_diff
        qdt_ref[0, sl, :] = qc.T.astype(_BF16)
        kc = _rope_chunk(kd[:, sl], tabd_ref, DIFF_ROT // 2)
        for v in range(4):
            kd_ref[v, :, sl] = jnp.where(comp == v, kc, 0.0).astype(_BF16)
        vdt_ref[0, sl, :] = vd[:, sl].T.astype(_BF16)
    us = _dot(xb, w1_ref[:, base + 3 * seg:base + 4 * seg])
    for c in range(seg // LANE):
        us_ref[pl.ds(c, us.shape[0], stride=seg // LANE), :] = us[:, c * LANE:(c + 1) * LANE]
    gbase = base + 4 * seg
    for c in range(N_BRANCH):
        sl = slice(c * d_model, (c + 1) * d_model)
        g = _dot(xb, w1_ref[:, gbase + c * d_model:gbase + (c + 1) * d_model]) + bg_ref[:, sl]
        gates_ref[:, sl] = jax.nn.sigmoid(g).astype(_BF16)


def _inproj(x2d, taba, tabd, w1, qn, wuq, kvn, wk, wv, bg, tm):
    t, d_model = x2d.shape
    nt = t // tm
    seg = DIFF_HEADS * 2 * DIFF_HEAD_DIM
    hm = MLA_HEADS * LANE
    row = lambda w: pl.BlockSpec((tm, w), lambda i: (i, 0))
    colt = lambda w: pl.BlockSpec((1, w, tm), lambda i: (i, 0, 0))
    tab = pl.BlockSpec((3, tm, LANE), lambda i: (0, i, 0))
    kern = functools.partial(
        _inproj_kernel,
        q_scale_mla=float((MLA_NOPE + MLA_ROPE) ** -0.5 * LOG2E),
        q_scale_diff=float(DIFF_HEAD_DIM ** -0.5 * LOG2E),
        d_model=d_model)
    out_shapes = (
        jax.ShapeDtypeStruct((nt, hm, tm), _BF16), jax.ShapeDtypeStruct((t, hm), _BF16),
        jax.ShapeDtypeStruct((nt, MLA_HEADS * MLA_V, tm), _BF16),
        jax.ShapeDtypeStruct((nt, seg, tm), _BF16), jax.ShapeDtypeStruct((4, t, seg), _BF16),
        jax.ShapeDtypeStruct((nt, seg, tm), _BF16), jax.ShapeDtypeStruct((t * (seg // LANE), LANE), _F32),
        jax.ShapeDtypeStruct((t, N_BRANCH * d_model), _BF16))
    out_specs = (colt(hm), row(hm), colt(MLA_HEADS * MLA_V), colt(seg),
                 pl.BlockSpec((4, tm, seg), lambda i: (0, i, 0)), colt(seg),
                 pl.BlockSpec((tm * (seg // LANE), LANE), lambda i: (i, 0)), row(N_BRANCH * d_model))
    return pl.pallas_call(
        kern,
        grid=(nt,),
        in_specs=[row(d_model), tab, tab, _const_spec(w1.shape), _const_spec(qn.shape), _const_spec(wuq.shape),
                  _const_spec(kvn.shape), _const_spec(wk.shape), _const_spec(wv.shape), _const_spec(bg.shape)],
        out_specs=out_specs,
        out_shape=out_shapes,
        compiler_params=_cparams(1),
        name="inproj",
    )(x2d, taba, tabd, w1, qn, wuq, kvn, wk, wv, bg)


def _flash_maps(q_ts, load_ks, load_vts, qi, tq, s_scrs):
    n = len(q_ts)
    a_scrs = s_scrs[n:]

    def qk(i, j):
        s_scrs[i][...] = _dot(load_ks[i](j), q_ts[i])

    def step(j, carry, diagonal):
        if diagonal:
            r = lax.broadcasted_iota(jnp.int32, (tq, tq), 0)
            c = lax.broadcasted_iota(jnp.int32, (tq, tq), 1)
            keep = r <= c
        out = []
        for i in range(n):
            if i + 1 < n:
                qk(i + 1, j)
            elif not diagonal:
                qk(0, j + 1)
            m, l = carry[i]
            s = s_scrs[i][...]
            if diagonal:
                s = jnp.where(keep, s, NEG_BIG)
            m_new = jnp.maximum(m, jnp.max(s, axis=0, keepdims=True))
            alpha = jnp.exp2(m - m_new)
            p = jnp.exp2(s - m_new)
            l = alpha * l + jnp.sum(p, axis=0, keepdims=True)
            a_scrs[i][...] = alpha * a_scrs[i][...] + _dot(load_vts[i](j), p.astype(_BF16))
            out.append((m_new, l))
        return tuple(out)

    init = tuple((jnp.full((1, tq), NEG_BIG, _F32), jnp.zeros((1, tq), _F32)) for _ in range(n))
    for i in range(n):
        a_scrs[i][...] = jnp.zeros_like(a_scrs[i])
    qk(0, 0)
    carry = lax.fori_loop(0, qi, lambda j, c: step(j, c, False), init)
    carry = step(qi, carry, True)
    return [a_scrs[i][...] / carry[i][1] for i in range(n)]


def _key_block(k_ref, lead, j, tq, lanes):
    return k_ref[lead + (pl.ds(pl.multiple_of(j * tq, tq), tq), lanes)]


def _mla_attn_kernel(qt_ref, k_ref, vt_ref, o_ref, *s_scrs, tq):
    qi = pl.program_id(2)
    npair = ATTN_PAIRS_PER_STEP
    sls = [slice(hh * LANE, (hh + 1) * LANE) for hh in range(2 * npair)]
    outs = _flash_maps([qt_ref[0, sl, :] for sl in sls],
                       [lambda j, sl=sl: _key_block(k_ref, (0,), j, tq, sl) for sl in sls],
                       [lambda j, pr=pr: vt_ref[0, j, pr * LANE:(pr + 1) * LANE, :]
                        for pr in range(npair) for _ in range(2)],
                       qi, tq, s_scrs)
    for pr in range(npair):
        ot = jnp.concatenate([outs[2 * pr][:MLA_V], outs[2 * pr + 1][MLA_V:]], axis=0)
        o_ref[0, :, pr * LANE:(pr + 1) * LANE] = ot.T.astype(o_ref.dtype)


def _mla_attention(qt, k, vt, b, tq):
    s = k.shape[1]
    nq = s // tq
    npair = ATTN_PAIRS_PER_STEP
    pairs = MLA_HEADS // 2
    assert pairs % npair == 0
    vt4 = vt.reshape(b, nq, vt.shape[1], tq)
    return pl.pallas_call(
        functools.partial(_mla_attn_kernel, tq=tq),
        grid=(b, pairs // npair, nq),
        in_specs=[pl.BlockSpec((1, 2 * npair * LANE, tq), lambda bi, p, qi: (bi * nq + qi, p, 0)),
                  pl.BlockSpec((1, s, 2 * npair * LANE), lambda bi, p, qi: (bi, 0, p)),
                  pl.BlockSpec((1, nq, npair * LANE, tq), lambda bi, p, qi: (bi, 0, p, 0))],
        out_specs=pl.BlockSpec((1, tq, npair * LANE), lambda bi, p, qi: (bi, qi, p)),
        out_shape=jax.ShapeDtypeStruct((b, s, pairs * LANE), _BF16),
        scratch_shapes=([pltpu.VMEM((tq, tq), _F32)] * (2 * npair)
                        + [pltpu.VMEM((LANE, tq), _F32)] * (2 * npair)),
        compiler_params=_cparams(3),
        name="mla_attention",
    )(qt, k, vt4)


def _diff_attn_kernel(qt_ref, k_ref, vt_ref, lam_ref, subln_ref, o_ref, *s_scrs, tq, lambda_init):
    qi = pl.program_id(2)
    npair = ATTN_PAIRS_PER_STEP
    lanes = [slice(pr * LANE, (pr + 1) * LANE) for pr in range(npair)]
    outs = _flash_maps([qt_ref[0, lanes[pr], :] for pr in range(npair) for _ in range(4)],
                       [lambda j, v=v, pr=pr: _key_block(k_ref, (v, 0), j, tq, lanes[pr])
                        for pr in range(npair) for v in range(4)],
                       [lambda j, pr=pr: vt_ref[0, j, lanes[pr], :] for pr in range(npair) for _ in range(4)],
                       qi, tq, s_scrs)
    lf = lam_ref[...]
    lam = (jnp.exp(jnp.sum(lf[0:1] * lf[1:2], axis=-1, keepdims=True))
           - jnp.exp(jnp.sum(lf[2:3] * lf[3:4], axis=-1, keepdims=True)) + lambda_init)
    half = 2 * DIFF_HEAD_DIM
    lane = lax.broadcasted_iota(jnp.int32, (tq, LANE), 1)
    first = lane < half
    for pr in range(npair):
        o = outs[4 * pr:4 * pr + 4]
        dt = jnp.concatenate([(o[0] - lam * o[1])[:half], (o[2] - lam * o[3])[half:]], axis=0)
        d = dt.T
        sq = d * d
        ss_a = jnp.sum(jnp.where(first, sq, 0.0), axis=-1, keepdims=True)
        ss_b = jnp.sum(jnp.where(first, 0.0, sq), axis=-1, keepdims=True)
        ms = jnp.where(first, ss_a, ss_b) * (1.0 / half)
        o_ref[0, :, lanes[pr]] = (d * lax.rsqrt(ms + SUBLN_EPS) * subln_ref[...]
                                  * (1.0 - lambda_init)).astype(o_ref.dtype)


def _diff_attention(qt, kvar, vt, lam, subln2, b, tq, lambda_init):
    s = kvar.shape[2]
    nq = s // tq
    npair = ATTN_PAIRS_PER_STEP
    pairs = DIFF_HEADS // 2
    assert pairs % npair == 0
    vt4 = vt.reshape(b, nq, vt.shape[1], tq)
    return pl.pallas_call(
        functools.partial(_diff_attn_kernel, tq=tq, lambda_init=lambda_init),
        grid=(b, pairs // npair, nq),
        in_specs=[pl.BlockSpec((1, npair * LANE, tq), lambda bi, p, qi: (bi * nq + qi, p, 0)),
                  pl.BlockSpec((4, 1, s, npair * LANE), lambda bi, p, qi: (0, bi, 0, p)),
                  pl.BlockSpec((1, nq, npair * LANE, tq), lambda bi, p, qi: (bi, 0, p, 0)),
                  _const_spec(lam.shape), _const_spec(subln2.shape)],
        out_specs=pl.BlockSpec((1, tq, npair * LANE), lambda bi, p, qi: (bi, qi, p)),
        out_shape=jax.ShapeDtypeStruct((b, s, pairs * LANE), _BF16),
        scratch_shapes=([pltpu.VMEM((tq, tq), _F32)] * (4 * npair)
                        + [pltpu.VMEM((LANE, tq), _F32)] * (4 * npair)),
        compiler_params=_cparams(3),
        name="diff_attention",
    )(qt, kvar, vt4, lam, subln2)


def _ssm_pack_kernel(x_ref, o_ref, *, ct, lane_tiles):
    L, P = SSM_CHUNK, SSM_GROUP
    per_lane_tile = LANE // P
    for s_in in range(L):
        for lt in range(lane_tiles):
            xs = x_ref[pl.ds(s_in * lane_tiles + lt, ct, stride=L * lane_tiles), :]
            xt = xs.T
            for gi in range(per_lane_tile):
                o_ref[lt * per_lane_tile + gi, s_in * P:(s_in + 1) * P, :] = (
                    xt[gi * P:(gi + 1) * P, :].astype(o_ref.dtype))


def _ssm_unpack_kernel(y_ref, o_ref, *, ct, lane_tiles):
    L, P = SSM_CHUNK, SSM_GROUP
    per_lane_tile = LANE // P
    for t_in in range(L):
        for lt in range(lane_tiles):
            z = jnp.concatenate([y_ref[lt * per_lane_tile + gi, t_in * P:(t_in + 1) * P, :].astype(_F32)
                                 for gi in range(per_lane_tile)], axis=0)
            o_ref[pl.ds(t_in * lane_tiles + lt, ct, stride=L * lane_tiles), :] = z.T


def _ssm_tiles(t, batch):
    chunks = t // batch // SSM_CHUNK
    ct = min(LANE, chunks)
    assert chunks % ct == 0
    return chunks, ct


def _ssm_pack(us, batch, groups):
    lane_tiles = groups * SSM_GROUP // LANE
    t = us.shape[0] // lane_tiles
    chunks, ct = _ssm_tiles(t, batch)
    return pl.pallas_call(
        functools.partial(_ssm_pack_kernel, ct=ct, lane_tiles=lane_tiles),
        grid=(t // (ct * SSM_CHUNK),),
        in_specs=[pl.BlockSpec((ct * SSM_CHUNK * lane_tiles, LANE), lambda i: (i, 0))],
        out_specs=pl.BlockSpec((groups, SSM_CHUNK * SSM_GROUP, ct), lambda i: (0, 0, i)),
        out_shape=jax.ShapeDtypeStruct((groups, SSM_CHUNK * SSM_GROUP, batch * chunks), _BF16),
        compiler_params=_cparams(1),
        name="ssm_pack",
    )(us)


def _ssm_unpack(yt, batch):
    groups, w, r = yt.shape
    t = r * SSM_CHUNK
    chunks, ct = _ssm_tiles(t, batch)
    lane_tiles = groups * SSM_GROUP // LANE
    return pl.pallas_call(
        functools.partial(_ssm_unpack_kernel, ct=ct, lane_tiles=lane_tiles),
        grid=(r // ct,),
        in_specs=[pl.BlockSpec((groups, w, ct), lambda i: (0, 0, i))],
        out_specs=pl.BlockSpec((ct * SSM_CHUNK * lane_tiles, LANE), lambda i: (i, 0)),
        out_shape=jax.ShapeDtypeStruct((t * lane_tiles, LANE), _F32),
        compiler_params=_cparams(1),
        name="ssm_unpack",
    )(yt)


def _ssm_kernel(u_ref, t_ref, b_ref, c_ref, ar_ref, ai_ref, d_ref, y_ref, x_scr, xs_scr, h_scr, *, batch, chunks):
    ut = u_ref[0]
    xt = _dot(b_ref[0], ut)
    x_scr[...] = xt.T
    xs_scr[...] = jnp.concatenate([xt[SSM_STATE:], xt[:SSM_STATE]], axis=0).T
    ar, ai = ar_ref[0], ai_ref[0]
    ai_s = pltpu.roll(ai, SSM_STATE, 1)

    def step(c, carry):
        h, hs = carry
        rows = pl.ds(c, batch, stride=chunks)
        h_scr[rows, :] = h
        return ar * h + ai * hs + x_scr[rows, :], ar * hs + ai_s * h + xs_scr[rows, :]

    zero = jnp.zeros((batch, 2 * SSM_STATE), _F32)
    lax.fori_loop(0, chunks, step, (zero, zero), unroll=SCAN_UNROLL)
    ht = h_scr[...].T.astype(_BF16)
    yt = _dot(t_ref[0], ut) + _dot(c_ref[0], ht) + ut.astype(_F32) * d_ref[0]
    y_ref[0] = jax.nn.gelu(yt).astype(y_ref.dtype)


def _ssm(u_t, tmat_t, bmat_t, cmat_t, ar, ai, d_col, batch):
    g, w, r = u_t.shape
    per_g = lambda shape: pl.BlockSpec((1,) + shape, lambda i: (i, 0, 0))
    return pl.pallas_call(
        functools.partial(_ssm_kernel, batch=batch, chunks=r // batch),
        grid=(g,),
        in_specs=[per_g((w, r)), per_g((w, w)), per_g((2 * SSM_STATE, w)), per_g((w, 2 * SSM_STATE)),
                  per_g((1, 2 * SSM_STATE)), per_g((1, 2 * SSM_STATE)), per_g((w, 1))],
        out_specs=per_g((w, r)),
        out_shape=jax.ShapeDtypeStruct((g, w, r), _BF16),
        scratch_shapes=[pltpu.VMEM((r, 2 * SSM_STATE), _F32)] * 3,
        compiler_params=_cparams(1),
        name="ssm_scan",
    )(u_t, tmat_t, bmat_t, cmat_t, ar, ai, d_col)


def _ssm_matrices(a_re, a_im, log_dt, b_re, b_im, c_re, c_im, d):
    L, P, N = SSM_CHUNK, SSM_GROUP, SSM_STATE
    G = a_re.shape[0]
    A = lax.complex(a_re.astype(_F32), a_im.astype(_F32))
    dt = jnp.exp(log_dt.astype(_F32))[:, None]
    a_bar = jnp.exp(A * dt)
    b_bar = ((a_bar - 1.0) / A)[..., None] * lax.complex(b_re.astype(_F32), b_im.astype(_F32))
    C = lax.complex(c_re.astype(_F32), c_im.astype(_F32))
    steps = jnp.arange(L + 1, dtype=_F32)
    apow = jnp.exp((A * dt)[None] * steps[:, None, None])
    m = jnp.real(jnp.einsum('gpn,dgn,gnq->dgpq', C, apow[:L], b_bar))
    s_idx = jnp.arange(L)[:, None]
    t_idx = jnp.arange(L)[None, :]
    lag = jnp.clip(t_idx - s_idx, 0, L - 1)
    tfull = jnp.where((t_idx >= s_idx)[:, :, None, None, None], m[lag], 0.0)
    tmat = tfull.transpose(2, 0, 4, 1, 3).reshape(G, L * P, L * P)
    bm = apow[L - 1 - jnp.arange(L)][:, :, :, None] * b_bar[None]
    bm = bm.transpose(1, 0, 3, 2).reshape(G, L * P, N)
    bmat = jnp.concatenate([jnp.real(bm), jnp.imag(bm)], axis=-1)
    cm = C[None] * apow[1:L + 1][:, :, None, :]
    cm = cm.transpose(1, 3, 0, 2).reshape(G, N, L * P)
    cmat = jnp.concatenate([jnp.real(cm), -jnp.imag(cm)], axis=1)
    al = apow[L]
    ar = jnp.concatenate([jnp.real(al), jnp.real(al)], axis=-1)[:, None, :]
    ai = jnp.concatenate([-jnp.imag(al), jnp.imag(al)], axis=-1)[:, None, :]
    dflat = jnp.tile(d.astype(_F32).reshape(G, 1, P), (1, L, 1)).reshape(G, 1, L * P)
    tr = lambda a: a.transpose(0, 2, 1).astype(_BF16)
    return tr(tmat), tr(bmat), tr(cmat), ar, ai, dflat.reshape(G, L * P, 1)


def _layernorm(z, g, b):
    mu = jnp.mean(z, axis=-1, keepdims=True)
    zc = z - mu
    var = jnp.mean(zc * zc, axis=-1, keepdims=True)
    return zc * lax.rsqrt(var + LN_EPS) * g + b


def _merge_kernel(om_ref, od_ref, ys_ref, g_ref, x_ref, wmo_ref, wdo_ref, wglu_ref, wo_ref, lng_ref, lnb_ref,
                  rw_ref, rb_ref, x1_ref, idx_ref, wts_ref, cnt_ref, *, alpha, d_model):
    def branch_dots(rows):
        n_rows = rows.stop - rows.start
        lt = wglu_ref.shape[0] // LANE
        ys = jnp.concatenate([ys_ref[pl.ds(rows.start * lt + c, n_rows, stride=lt), :] for c in range(lt)], axis=1)
        return (_dot(om_ref[rows, :], wmo_ref[...]), _dot(od_ref[rows, :], wdo_ref[...]),
                _dot(ys.astype(_BF16), wglu_ref[...]))

    def gate_merge(rows, ya, yb, gl):
        yc = gl[:, :d_model] * jax.nn.sigmoid(gl[:, d_model:])
        merged = (g_ref[rows, 0:d_model].astype(_F32) * ya + g_ref[rows, d_model:2 * d_model].astype(_F32) * yb
                  + g_ref[rows, 2 * d_model:3 * d_model].astype(_F32) * yc)
        return merged.astype(_BF16)

    def out_norm(rows, merged):
        z = alpha * x_ref[rows, :] + _dot(merged, wo_ref[...])
        x1 = _layernorm(z, lng_ref[...], lnb_ref[...])
        for c in range(d_model // LANE):
            x1_ref[pl.ds(rows.start * TOKEN_TILE + c, rows.stop - rows.start, stride=TOKEN_TILE), :] = (
                x1[:, c * LANE:(c + 1) * LANE])
        return x1

    def router_logits(x1):
        return _dot_nt(rw_ref[...], x1, precision=lax.Precision.HIGHEST)

    def route(rows, logits):
        scores = jax.nn.sigmoid(logits)
        biased = scores + rb_ref[...]
        epg = logits.shape[0] // N_GROUPS
        a = [biased[j * N_GROUPS:(j + 1) * N_GROUPS] for j in range(epg)]
        sc = [scores[j * N_GROUPS:(j + 1) * N_GROUPS] for j in range(epg)]
        gs = None
        for i in range(epg):
            for j in range(i + 1, epg):
                pair = a[i] + a[j]
                gs = pair if gs is None else jnp.maximum(gs, pair)
        giota = lax.broadcasted_iota(jnp.int32, gs.shape, 0)
        gmax = jnp.max(gs, axis=0, keepdims=True)
        gsel = jnp.min(jnp.where(gs == gmax, giota, N_GROUPS), axis=0, keepdims=True)
        hot = giota == gsel
        val = [jnp.sum(jnp.where(hot, a[j], 0.0), axis=0, keepdims=True) for j in range(epg)]
        raw = [jnp.sum(jnp.where(hot, sc[j], 0.0), axis=0, keepdims=True) for j in range(epg)]
        b1, i1, w1 = val[0], jnp.zeros_like(gsel), raw[0]
        for j in range(1, epg):
            take = val[j] > b1
            b1 = jnp.where(take, val[j], b1)
            i1 = jnp.where(take, j, i1)
            w1 = jnp.where(take, raw[j], w1)
        b2 = jnp.full_like(b1, -jnp.inf)
        i2 = jnp.zeros_like(gsel)
        w2 = jnp.zeros_like(w1)
        for j in range(epg):
            take = (i1 != j) & (val[j] > b2)
            b2 = jnp.where(take, val[j], b2)
            i2 = jnp.where(take, j, i2)
            w2 = jnp.where(take, raw[j], w2)
        wsum = w1 + w2
        e1, e2 = gsel * epg + i1, gsel * epg + i2
        idx_ref[:, rows] = jnp.concatenate([e1, e2], axis=0)
        wts_ref[:, rows] = jnp.concatenate([w1 / wsum, w2 / wsum], axis=0)
        eiota = lax.broadcasted_iota(jnp.int32, logits.shape, 0)
        hit = (eiota == e1).astype(jnp.int32) + (eiota == e2).astype(jnp.int32)
        part = hit[:, 0:LANE]
        for c in range(1, hit.shape[1] // LANE):
            part = part + hit[:, c * LANE:(c + 1) * LANE]
        return part

    tm = x_ref.shape[0]
    sub = tm // MERGE_SUBTILES
    rows = [slice(k * sub, (k + 1) * sub) for k in range(MERGE_SUBTILES)]
    dots = [branch_dots(r) for r in rows]
    merged = [gate_merge(r, *d) for r, d in zip(rows, dots)]
    x1s = [out_norm(r, m) for r, m in zip(rows, merged)]
    logits = [router_logits(x1) for x1 in x1s]
    parts = [route(r, lg) for r, lg in zip(rows, logits)]

    @pl.when(pl.program_id(0) == 0)
    def _():
        cnt_ref[...] = jnp.zeros_like(cnt_ref)

    cnt_ref[...] += functools.reduce(lambda p, q: p + q, parts)


def _merge(om, od, ys, gates, x2d, wmo, wdo, wglu, wo, lng, lnb, rw, rb, tm, alpha):
    t, d_model = x2d.shape
    n_experts = rw.shape[0]
    assert d_model == TOKEN_TILE * LANE and tm % MERGE_SUBTILES == 0
    row = lambda w: pl.BlockSpec((tm, w), lambda i: (i, 0))
    col = pl.BlockSpec((TOP_K, tm), lambda i: (0, i))
    return pl.pallas_call(
        functools.partial(_merge_kernel, alpha=alpha, d_model=d_model),
        grid=(t // tm,),
        in_specs=[row(om.shape[1]), row(od.shape[1]),
                  pl.BlockSpec((tm * (ys.shape[0] // t), LANE), lambda i: (i, 0)), row(gates.shape[1]), row(d_model),
                  _const_spec(wmo.shape), _const_spec(wdo.shape), _const_spec(wglu.shape), _const_spec(wo.shape),
                  _const_spec(lng.shape), _const_spec(lnb.shape), _const_spec(rw.shape), _const_spec(rb.shape)],
        out_specs=(pl.BlockSpec((tm * TOKEN_TILE, LANE), lambda i: (i, 0)), col, col,
                   pl.BlockSpec((n_experts, LANE), lambda i: (0, 0))),
        out_shape=(jax.ShapeDtypeStruct((t * TOKEN_TILE, LANE), _F32), jax.ShapeDtypeStruct((TOP_K, t), jnp.int32),
                   jax.ShapeDtypeStruct((TOP_K, t), _F32), jax.ShapeDtypeStruct((n_experts, LANE), jnp.int32)),
        compiler_params=_cparams(1),
        name="merge_router",
    )(om, od, ys, gates, x2d, wmo, wdo, wglu, wo, lng, lnb, rw, rb)


DMA_UNROLL = 8


def _moe_kernel(be_ref, cnt_ref, nu_ref, tok_ref, tokn_ref, dst_ref, x_hbm, wg_ref, wu_ref, wd_ref, out_hbm,
                xbuf, ybuf, wgb, wub, wdb, sem_in, sem_out, *, mb):
    i = pl.program_id(0)
    n_used = nu_ref[0]
    slot = lax.rem(i, 2)

    def for_rows(start_row):
        def body(j, carry):
            for u in range(DMA_UNROLL):
                start_row(j * DMA_UNROLL + u, u % 2)
            return carry
        lax.fori_loop(0, mb // DMA_UNROLL, body, 0)

    def for_first_rows(n, fn):
        def body(r, carry):
            fn(r)
            return carry
        lax.fori_loop(0, n, body, 0)

    def tile_rows(tok):
        return pl.ds(pl.multiple_of(tok * TOKEN_TILE, TOKEN_TILE), TOKEN_TILE)

    def gather_row(t_ref, s, r):
        return pltpu.make_async_copy(x_hbm.at[tile_rows(t_ref[0, 0, r]), :], xbuf.at[s, tile_rows(r), :],
                                     sem_in.at[s])

    def scatter_row(s, r, dst):
        return pltpu.make_async_copy(ybuf.at[s, tile_rows(r), :], out_hbm.at[tile_rows(dst), :], sem_out.at[s])

    def issue_gather(t_ref, s):
        for_rows(lambda r, pr: gather_row(t_ref, s, r).start(priority=pr))

    def wait_gather(s):
        pltpu.make_async_copy(x_hbm.at[pl.ds(0, mb * TOKEN_TILE), :], xbuf.at[s], sem_in.at[s]).wait()

    def issue_scatter(s, cnt):
        @pl.when(cnt == mb)
        def _():
            for_rows(lambda r, pr: scatter_row(s, r, dst_ref[0, 0, r]).start(priority=pr))

        @pl.when(cnt < mb)
        def _():
            for_first_rows(cnt, lambda r: scatter_row(s, r, dst_ref[0, 0, r]).start())

    def wait_scatter(s, cnt):
        @pl.when(cnt == mb)
        def _():
            pltpu.make_async_copy(ybuf.at[s], out_hbm.at[pl.ds(0, mb * TOKEN_TILE), :], sem_out.at[s]).wait()

        @pl.when(cnt < mb)
        def _():
            for_first_rows(cnt, lambda r: scatter_row(s, r, 0).wait())

    @pl.when(i < n_used)
    def _():
        @pl.when(i == 0)
        def _():
            issue_gather(tok_ref, 0)

        @pl.when(i + 1 < n_used)
        def _():
            issue_gather(tokn_ref, 1 - slot)

        @pl.when((i == 0) | (be_ref[i] != be_ref[jnp.maximum(i - 1, 0)]))
        def _():
            wgb[...] = wg_ref[0, 0].astype(_BF16)
            wub[...] = wu_ref[0, 0].astype(_BF16)
            wdb[...] = wd_ref[0, 0].astype(_BF16)

        wait_gather(slot)

        @pl.when(i >= 2)
        def _():
            wait_scatter(slot, cnt_ref[jnp.maximum(i - 2, 0)])

        xb = jnp.concatenate([xbuf[slot, pl.ds(c, mb, stride=TOKEN_TILE), :] for c in range(TOKEN_TILE)],
                             axis=1).astype(_BF16)
        h = jax.nn.silu(_dot(xb, wgb[...])) * _dot(xb, wub[...])
        y = _dot(h.astype(_BF16), wdb[...])
        for c in range(TOKEN_TILE):
            ybuf[slot, pl.ds(c, mb, stride=TOKEN_TILE), :] = y[:, c * LANE:(c + 1) * LANE]
        issue_scatter(slot, cnt_ref[i])

        @pl.when(i == n_used - 1)
        def _():
            @pl.when(i >= 1)
            def _():
                wait_scatter(1 - slot, cnt_ref[jnp.maximum(i - 1, 0)])
            wait_scatter(slot, cnt_ref[i])


def _moe(block_e, block_cnt, n_used, row_tok, row_dst, x1t, wg, wu, wd, layer):
    t = x1t.shape[0] // TOKEN_TILE
    d_model = TOKEN_TILE * LANE
    n_blocks, mb = row_tok.shape
    ff = wg.shape[-1]
    assert mb % DMA_UNROLL == 0
    smem_row = lambda f: pl.BlockSpec((1, 1, mb), lambda i, be, cnt, nu: (f(i), 0, 0), memory_space=pltpu.SMEM)
    wspec = lambda a, b: pl.BlockSpec((1, 1, a, b), lambda i, be, cnt, nu: (layer, be[i], 0, 0))
    grid_spec = pltpu.PrefetchScalarGridSpec(
        num_scalar_prefetch=3,
        grid=(n_blocks,),
        in_specs=[smem_row(lambda i: i), smem_row(lambda i: jnp.minimum(i + 1, n_blocks - 1)), smem_row(lambda i: i),
                  pl.BlockSpec(memory_space=pl.ANY),
                  wspec(d_model, ff), wspec(d_model, ff), wspec(ff, d_model)],
        out_specs=pl.BlockSpec(memory_space=pl.ANY),
        scratch_shapes=[pltpu.VMEM((2, mb * TOKEN_TILE, LANE), _F32), pltpu.VMEM((2, mb * TOKEN_TILE, LANE), _F32),
                        pltpu.VMEM((d_model, ff), _BF16), pltpu.VMEM((d_model, ff), _BF16),
                        pltpu.VMEM((ff, d_model), _BF16),
                        pltpu.SemaphoreType.DMA((2,)), pltpu.SemaphoreType.DMA((2,))])
    tok3 = row_tok.reshape(n_blocks, 1, mb)
    return pl.pallas_call(
        functools.partial(_moe_kernel, mb=mb),
        grid_spec=grid_spec,
        out_shape=jax.ShapeDtypeStruct((TOP_K * t * TOKEN_TILE, LANE), _F32),
        compiler_params=_cparams(1),
        name="moe_experts",
    )(block_e, block_cnt, n_used, tok3, tok3, row_dst.reshape(n_blocks, 1, mb), x1t, wg, wu, wd)


def _moe_plan(idx, counts, mb):
    k, t = idx.shape
    a_total = k * t
    n_experts = counts.shape[0]
    e_flat = idx.reshape(a_total)
    order = jnp.argsort(e_flat).astype(jnp.int32)
    starts = jnp.cumsum(counts) - counts
    padded = (counts + mb - 1) // mb * mb
    pad_ends = jnp.cumsum(padded)
    pad_starts = pad_ends - padded
    n_blocks = -(-a_total // mb) + n_experts
    blk = jnp.arange(n_blocks, dtype=jnp.int32)
    blk_start = blk * mb
    block_e = jnp.minimum(jnp.sum((pad_ends[None, :] <= blk_start[:, None]).astype(jnp.int32), axis=1),
                          n_experts - 1)
    off0 = blk_start - pad_starts[block_e]
    cnt = jnp.clip(counts[block_e] - off0, 0, mb)
    n_used = jnp.sum((cnt > 0).astype(jnp.int32)).reshape(1)
    lane = jnp.arange(mb, dtype=jnp.int32)[None, :]
    valid = lane < cnt[:, None]
    src = jnp.clip((starts[block_e] + off0)[:, None] + lane, 0, a_total - 1)
    a_r = order[src]
    row_dst = jnp.where(valid, a_r, 0).astype(jnp.int32)
    row_tok = jnp.where(valid, a_r % t, 0).astype(jnp.int32)
    return block_e.astype(jnp.int32), cnt.astype(jnp.int32), n_used, row_tok, row_dst


def _final_kernel(x_ref, y0_ref, y1_ref, w_ref, lng_ref, lnb_ref, o_ref, *, alpha):
    tm = o_ref.shape[0]

    def tokens(ref):
        return jnp.concatenate([ref[pl.ds(c, tm, stride=TOKEN_TILE), :] for c in range(TOKEN_TILE)], axis=1)

    w = w_ref[...]
    z = alpha * tokens(x_ref) + (tokens(y0_ref) * w[:, 0:1] + tokens(y1_ref) * w[:, 1:2])
    o_ref[...] = _layernorm(z, lng_ref[...], lnb_ref[...])


def _final(x1t, yslots, wts_t, lng, lnb, tm, alpha):
    t = x1t.shape[0] // TOKEN_TILE
    d_model = TOKEN_TILE * LANE
    nt = t // tm
    tile = lambda f: pl.BlockSpec((tm * TOKEN_TILE, LANE), lambda i: (f(i), 0))
    return pl.pallas_call(
        functools.partial(_final_kernel, alpha=alpha),
        grid=(nt,),
        in_specs=[tile(lambda i: i), tile(lambda i: i), tile(lambda i: nt + i),
                  pl.BlockSpec((tm, TOP_K), lambda i: (i, 0)),
                  _const_spec(lng.shape), _const_spec(lnb.shape)],
        out_specs=pl.BlockSpec((tm, d_model), lambda i: (i, 0)),
        out_shape=jax.ShapeDtypeStruct((t, d_model), _F32),
        compiler_params=_cparams(1),
        name="moe_residual_ln",
    )(x1t, yslots, yslots, wts_t, lng, lnb)


def _prep_w_in(w):
    d_model = w.shape[0]
    seg = DIFF_HEADS * 2 * DIFF_HEAD_DIM
    o = 0
    cq = w[:, o:o + MLA_Q_LORA]; o += MLA_Q_LORA
    ckv = w[:, o:o + MLA_KV_LORA]; o += MLA_KV_LORA
    kr = w[:, o:o + MLA_ROPE]; o += MLA_ROPE
    rest = w[:, o:]
    z = lambda n: jnp.zeros((d_model, n), w.dtype)
    kr_chunk = jnp.concatenate([z(MLA_NOPE), kr, z(LANE - MLA_NOPE - MLA_ROPE)], axis=1)
    assert MLA_Q_LORA + MLA_KV_LORA + LANE == 4 * LANE and rest.shape[1] == 4 * seg + N_BRANCH * d_model
    return jnp.concatenate([cq, ckv, kr_chunk, rest], axis=1).astype(_BF16)


def _prep_w_uq(w):
    dq = MLA_NOPE + MLA_ROPE
    w3 = w.reshape(w.shape[0], MLA_HEADS, dq)
    w3 = jnp.pad(w3, ((0, 0), (0, 0), (0, LANE - dq)))
    return w3.reshape(w.shape[0], MLA_HEADS * LANE).astype(_BF16)


def _prep_w_ukv(w):
    w3 = w.reshape(w.shape[0], MLA_HEADS, MLA_NOPE + MLA_V)
    wk = jnp.pad(w3[:, :, :MLA_NOPE], ((0, 0), (0, 0), (0, LANE - MLA_NOPE)))
    wv = w3[:, :, MLA_NOPE:]
    return (wk.reshape(w.shape[0], MLA_HEADS * LANE).astype(_BF16),
            wv.reshape(w.shape[0], MLA_HEADS * MLA_V).astype(_BF16))


def kernel(x, positions, w_in, b_gate, mla_q_norm, mla_w_uq, mla_kv_norm, mla_w_ukv, mla_w_out,
           diff_lambda, diff_subln, diff_w_out, ssm_a_re, ssm_a_im, ssm_log_dt, ssm_b_re, ssm_b_im,
           ssm_c_re, ssm_c_im, ssm_d, ssm_w_glu, w_o, ln_gain, ln_bias, router_w, router_bias,
           moe_w_gate, moe_w_up, moe_w_down):
    B, S, D = x.shape
    T = B * S
    depth = w_in.shape[0]
    n_experts = router_w.shape[1]
    epg = n_experts // N_GROUPS
    alpha = float((2 * depth) ** 0.25)
    tm = min(ROW_TILE, T)
    tq = min(ATTN_TILE, S)
    L, P = SSM_CHUNK, SSM_GROUP
    G = ssm_a_re.shape[1]
    assert T % tm == 0 and S % tq == 0 and S % L == 0 and (TOP_K * T) % MOE_ROWS == 0
    assert MLA_HEADS * MLA_V == DIFF_HEADS * 2 * DIFF_HEAD_DIM == G * P

    (inv_a, sel_a), (inv_d, sel_d) = _rope_lane_patterns()
    pos_col = positions.reshape(T, 1).astype(jnp.int32)
    taba = _rope_tables(pos_col, inv_a, sel_a, tm)
    tabd = _rope_tables(pos_col, inv_d, sel_d, tm)

    rw = router_w.T.reshape(N_GROUPS, epg, D).transpose(1, 0, 2).reshape(n_experts, D).astype(_F32)
    rb = router_bias.reshape(N_GROUPS, epg).T.reshape(n_experts, 1).astype(_F32)

    x2d = x.reshape(T, D)
    for l in range(depth):
        lambda_init = 0.8 - 0.6 * math.exp(-0.3 * l)
        w1 = _prep_w_in(w_in[l])
        wuq = _prep_w_uq(mla_w_uq[l])
        wk, wv = _prep_w_ukv(mla_w_ukv[l])
        qmt, km, vmt, qdt, kdv, vdt, us, gates = _inproj(
            x2d, taba, tabd, w1, mla_q_norm[l][None].astype(_F32), wuq, mla_kv_norm[l][None].astype(_F32),
            wk, wv, b_gate[l].reshape(1, N_BRANCH * D).astype(_F32), tq)

        o_mla = _mla_attention(qmt, km.reshape(B, S, -1), vmt, B, tq)
        subln2 = jnp.tile(diff_subln[l].astype(_F32), 2)[None]
        o_diff = _diff_attention(qdt, kdv.reshape(4, B, S, -1), vdt, diff_lambda[l].astype(_F32), subln2,
                                 B, tq, lambda_init)

        mats = _ssm_matrices(ssm_a_re[l], ssm_a_im[l], ssm_log_dt[l], ssm_b_re[l], ssm_b_im[l],
                             ssm_c_re[l], ssm_c_im[l], ssm_d[l])
        ys = _ssm_unpack(_ssm(_ssm_pack(us, B, G), *mats, batch=B), B)

        x1, idx, wts, cnt_part = _merge(
            o_mla.reshape(T, -1), o_diff.reshape(T, -1), ys, gates, x2d,
            mla_w_out[l].astype(_BF16), diff_w_out[l].astype(_BF16), ssm_w_glu[l].astype(_BF16),
            w_o[l].astype(_BF16), ln_gain[l, 0][None].astype(_F32), ln_bias[l, 0][None].astype(_F32),
            rw, rb, min(MERGE_TILE, T), alpha)

        plan = _moe_plan(idx, jnp.sum(cnt_part, axis=1), MOE_ROWS)
        yslots = _moe(*plan, x1, moe_w_gate.astype(_F32), moe_w_up.astype(_F32), moe_w_down.astype(_F32), l)
        x2d = _final(x1, yslots, wts.T, ln_gain[l, 1][None].astype(_F32), ln_bias[l, 1][None].astype(_F32),
                     tm, alpha)
    return x2d.reshape(B, S, D)
```

```python
import functools
import math

import jax
import jax.numpy as jnp
from jax import lax
from jax.experimental import pallas as pl
from jax.experimental.pallas import tpu as pltpu

MLA_HEADS = 8
MLA_Q_LORA = 256
MLA_KV_LORA = 128
MLA_NOPE = 64
MLA_ROPE = 32
MLA_V = 64
DIFF_HEADS = 8
DIFF_HEAD_DIM = 32
DIFF_ROT = DIFF_HEAD_DIM // 4
SSM_GROUP = 16
SSM_STATE = 64
N_BRANCH = 3
ROPE_THETA = 500000.0
N_GROUPS = 8
TOP_K = 2
LN_EPS = 1e-5
RMS_EPS = 1e-6
SUBLN_EPS = 1e-5

LANE = 128
ROW_TILE = 512
ATTN_TILE = 512
MLA_PAIRS_PER_STEP = 4
DIFF_PAIRS_PER_STEP = 2
SSM_CHUNK = 16
SCAN_UNROLL = 4
MOE_ROWS = 512
MERGE_TILE = 512
MERGE_SUBTILES = 2
TOKEN_TILE = 8
VMEM_LIMIT = 56 * 1024 * 1024

NEG_BIG = -1e30
LOG2E = math.log2(math.e)

_F32 = jnp.float32
_BF16 = jnp.bfloat16


def _cparams(n_axes):
    return pltpu.CompilerParams(dimension_semantics=("arbitrary",) * n_axes,
                                vmem_limit_bytes=VMEM_LIMIT)


def _dot(a, b):
    return jnp.dot(a, b, preferred_element_type=_F32)


def _dot_nt(a, b, precision=None):
    return lax.dot_general(a, b, (((1,), (1,)), ((), ())), precision=precision,
                           preferred_element_type=_F32)


def _const_spec(shape):
    zeros = (0,) * len(shape)
    return pl.BlockSpec(shape, lambda *_: zeros, pipeline_mode=pl.Buffered(1))


def _rope_table_kernel(pos_ref, inv_ref, sel_ref, out_ref):
    pos = pos_ref[...].astype(_F32)
    ang = pos * inv_ref[...]
    c, s = jnp.cos(ang), jnp.sin(ang)
    sel = sel_ref[...]
    out_ref[0] = c * (sel[0:1] + sel[1:2]) + sel[2:3]
    out_ref[1] = -s * sel[0:1]
    out_ref[2] = s * sel[1:2]


def _rope_tables(pos_col, inv_row, sel_rows, tm):
    t = pos_col.shape[0]
    return pl.pallas_call(
        _rope_table_kernel,
        grid=(t // tm,),
        in_specs=[pl.BlockSpec((tm, 1), lambda i: (i, 0)), _const_spec((1, LANE)), _const_spec((3, LANE))],
        out_specs=pl.BlockSpec((3, tm, LANE), lambda i: (0, i, 0)),
        out_shape=jax.ShapeDtypeStruct((3, t, LANE), _F32),
        compiler_params=_cparams(1),
        name="rope_tables",
    )(pos_col, inv_row, sel_rows)


def _rope_lane_patterns():
    lane = jnp.arange(LANE)
    half_a = MLA_ROPE // 2
    in_rope = (lane >= MLA_NOPE) & (lane < MLA_NOPE + MLA_ROPE)
    fi = (lane - MLA_NOPE) % half_a
    inv_a = jnp.where(in_rope, ROPE_THETA ** (-(2.0 * fi.astype(_F32)) / MLA_ROPE), 0.0)
    x1_a = in_rope & (lane < MLA_NOPE + half_a)
    x2_a = in_rope & ~x1_a
    sel_a = jnp.stack([x1_a, x2_a, ~in_rope]).astype(_F32)
    half_d = DIFF_ROT // 2
    d = lane % DIFF_HEAD_DIM
    in_rot = d < DIFF_ROT
    inv_d = jnp.where(in_rot, ROPE_THETA ** (-(2.0 * (d % half_d).astype(_F32)) / DIFF_ROT), 0.0)
    x1_d = d < half_d
    x2_d = in_rot & ~x1_d
    sel_d = jnp.stack([x1_d, x2_d, ~in_rot]).astype(_F32)
    return (inv_a[None].astype(_F32), sel_a), (inv_d[None].astype(_F32), sel_d)


def _rope_chunk(x, tab_ref, half):
    return (x * tab_ref[0] + pltpu.roll(x, LANE - half, 1) * tab_ref[1]
            + pltpu.roll(x, half, 1) * tab_ref[2])


def _rms(x, g, eps):
    return x * lax.rsqrt(jnp.mean(x * x, axis=-1, keepdims=True) + eps) * g


def _inproj_kernel(x_ref, taba_ref, tabd_ref, w1_ref, qn_ref, wuq_ref, kvn_ref, wk_ref, wv_ref, bg_ref,
                   qmt_ref, km_ref, vmt_ref, qdt_ref, kd_ref, vdt_ref, us_ref, gates_ref,
                   *, q_scale_mla, q_scale_diff, d_model):
    xb = x_ref[...].astype(_BF16)
    n_mla = MLA_HEADS
    lat = _dot(xb, w1_ref[:, 0:4 * LANE])
    cqn = _rms(lat[:, :MLA_Q_LORA], qn_ref[...], RMS_EPS).astype(_BF16)
    q = _dot(cqn, wuq_ref[...])
    for h in range(n_mla):
        sl = slice(h * LANE, (h + 1) * LANE)
        qh = _rope_chunk(q[:, sl], taba_ref, MLA_ROPE // 2) * q_scale_mla
        qmt_ref[0, sl, :] = qh.T.astype(_BF16)
    ckvn = _rms(lat[:, MLA_Q_LORA:MLA_Q_LORA + MLA_KV_LORA], kvn_ref[...], RMS_EPS).astype(_BF16)
    kn = _dot(ckvn, wk_ref[...])
    kr = _rope_chunk(lat[:, 3 * LANE:4 * LANE], taba_ref, MLA_ROPE // 2)
    for h in range(n_mla):
        sl = slice(h * LANE, (h + 1) * LANE)
        km_ref[:, sl] = (kn[:, sl] + kr).astype(_BF16)
    vm = _dot(ckvn, wv_ref[...])
    seg = DIFF_HEADS * 2 * DIFF_HEAD_DIM
    for c in range(seg // LANE):
        sl = slice(c * LANE, (c + 1) * LANE)
        vmt_ref[0, sl, :] = vm[:, sl].T.astype(_BF16)

    base = 4 * LANE
    lane = lax.broadcasted_iota(jnp.int32, (1, LANE), 1)
    comp = lane // DIFF_HEAD_DIM
    qd = _dot(xb, w1_ref[:, base:base + seg])
    kd = _dot(xb, w1_ref[:, base + seg:base + 2 * seg])
    vd = _dot(xb, w1_ref[:, base + 2 * seg:base + 3 * seg])
    for c in range(seg // LANE):
        sl = slice(c * LANE, (c + 1) * LANE)
        qc = _rope_chunk(qd[:, sl], tabd_ref, DIFF_ROT // 2) * q_scale_diff
        qdt_ref[0, sl, :] = qc.T.astype(_BF16)
        kc = _rope_chunk(kd[:, sl], tabd_ref, DIFF_ROT // 2)
        for v in range(4):
            kd_ref[v, :, sl] = jnp.where(comp == v, kc, 0.0).astype(_BF16)
        vdt_ref[0, sl, :] = vd[:, sl].T.astype(_BF16)
    us = _dot(xb, w1_ref[:, base + 3 * seg:base + 4 * seg])
    for c in range(seg // LANE):
        us_ref[pl.ds(c, us.shape[0], stride=seg // LANE), :] = us[:, c * LANE:(c + 1) * LANE]
    gbase = base + 4 * seg
    for c in range(N_BRANCH):
        sl = slice(c * d_model, (c + 1) * d_model)
        g = _dot(xb, w1_ref[:, gbase + c * d_model:gbase + (c + 1) * d_model]) + bg_ref[:, sl]
        gates_ref[:, sl] = jax.nn.sigmoid(g).astype(_BF16)


def _inproj(x2d, taba, tabd, w1, qn, wuq, kvn, wk, wv, bg, tm):
    t, d_model = x2d.shape
    nt = t // tm
    seg = DIFF_HEADS * 2 * DIFF_HEAD_DIM
    hm = MLA_HEADS * LANE
    row = lambda w: pl.BlockSpec((tm, w), lambda i: (i, 0))
    colt = lambda w: pl.BlockSpec((1, w, tm), lambda i: (i, 0, 0))
    tab = pl.BlockSpec((3, tm, LANE), lambda i: (0, i, 0))
    kern = functools.partial(
        _inproj_kernel,
        q_scale_mla=float((MLA_NOPE + MLA_ROPE) ** -0.5 * LOG2E),
        q_scale_diff=float(DIFF_HEAD_DIM ** -0.5 * LOG2E),
        d_model=d_model)
    out_shapes = (
        jax.ShapeDtypeStruct((nt, hm, tm), _BF16), jax.ShapeDtypeStruct((t, hm), _BF16),
        jax.ShapeDtypeStruct((nt, MLA_HEADS * MLA_V, tm), _BF16),
        jax.ShapeDtypeStruct((nt, seg, tm), _BF16), jax.ShapeDtypeStruct((4, t, seg), _BF16),
        jax.ShapeDtypeStruct((nt, seg, tm), _BF16), jax.ShapeDtypeStruct((t * (seg // LANE), LANE), _F32),
        jax.ShapeDtypeStruct((t, N_BRANCH * d_model), _BF16))
    out_specs = (colt(hm), row(hm), colt(MLA_HEADS * MLA_V), colt(seg),
                 pl.BlockSpec((4, tm, seg), lambda i: (0, i, 0)), colt(seg),
                 pl.BlockSpec((tm * (seg // LANE), LANE), lambda i: (i, 0)), row(N_BRANCH * d_model))
    return pl.pallas_call(
        kern,
        grid=(nt,),
        in_specs=[row(d_model), tab, tab, _const_spec(w1.shape), _const_spec(qn.shape), _const_spec(wuq.shape),
                  _const_spec(kvn.shape), _const_spec(wk.shape), _const_spec(wv.shape), _const_spec(bg.shape)],
        out_specs=out_specs,
        out_shape=out_shapes,
        compiler_params=_cparams(1),
        name="inproj",
    )(x2d, taba, tabd, w1, qn, wuq, kvn, wk, wv, bg)


def _flash_maps(q_ts, load_ks, load_vts, qi, tq, s_scrs):
    n = len(q_ts)
    a_scrs = s_scrs[n:]

    def qk(i, j):
        s_scrs[i][...] = _dot(load_ks[i](j), q_ts[i])

    def step(j, carry, diagonal):
        if diagonal:
            r = lax.broadcasted_iota(jnp.int32, (tq, tq), 0)
            c = lax.broadcasted_iota(jnp.int32, (tq, tq), 1)
            keep = r <= c
        out = []
        for i in range(n):
            if i + 1 < n:
                qk(i + 1, j)
            elif not diagonal:
                qk(0, j + 1)
            m, l = carry[i]
            s = s_scrs[i][...]
            if diagonal:
                s = jnp.where(keep, s, NEG_BIG)
            m_new = jnp.maximum(m, jnp.max(s, axis=0, keepdims=True))
            alpha = jnp.exp2(m - m_new)
            p = jnp.exp2(s - m_new)
            l = alpha * l + jnp.sum(p, axis=0, keepdims=True)
            a_scrs[i][...] = alpha * a_scrs[i][...] + _dot(load_vts[i](j), p.astype(_BF16))
            out.append((m_new, l))
        return tuple(out)

    init = tuple((jnp.full((1, tq), NEG_BIG, _F32), jnp.zeros((1, tq), _F32)) for _ in range(n))
    for i in range(n):
        a_scrs[i][...] = jnp.zeros_like(a_scrs[i])
    qk(0, 0)
    carry = lax.fori_loop(0, qi, lambda j, c: step(j, c, False), init)
    carry = step(qi, carry, True)
    return [a_scrs[i][...] / carry[i][1] for i in range(n)]


def _key_block(k_ref, lead, j, tq, lanes):
    return k_ref[lead + (pl.ds(pl.multiple_of(j * tq, tq), tq), lanes)]


def _mla_attn_kernel(qt_ref, k_ref, vt_ref, o_ref, *s_scrs, tq):
    qi = pl.program_id(2)
    npair = MLA_PAIRS_PER_STEP
    sls = [slice(hh * LANE, (hh + 1) * LANE) for hh in range(2 * npair)]
    outs = _flash_maps([qt_ref[0, sl, :] for sl in sls],
                       [lambda j, sl=sl: _key_block(k_ref, (0,), j, tq, sl) for sl in sls],
                       [lambda j, pr=pr: vt_ref[0, j, pr * LANE:(pr + 1) * LANE, :]
                        for pr in range(npair) for _ in range(2)],
                       qi, tq, s_scrs)
    for pr in range(npair):
        ot = jnp.concatenate([outs[2 * pr][:MLA_V], outs[2 * pr + 1][MLA_V:]], axis=0)
        o_ref[0, :, pr * LANE:(pr + 1) * LANE] = ot.T.astype(o_ref.dtype)


def _mla_attention(qt, k, vt, b, tq):
    s = k.shape[1]
    nq = s // tq
    npair = MLA_PAIRS_PER_STEP
    pairs = MLA_HEADS // 2
    assert pairs % npair == 0
    vt4 = vt.reshape(b, nq, vt.shape[1], tq)
    return pl.pallas_call(
        functools.partial(_mla_attn_kernel, tq=tq),
        grid=(b, pairs // npair, nq),
        in_specs=[pl.BlockSpec((1, 2 * npair * LANE, tq), lambda bi, p, qi: (bi * nq + qi, p, 0)),
                  pl.BlockSpec((1, s, 2 * npair * LANE), lambda bi, p, qi: (bi, 0, p)),
                  pl.BlockSpec((1, nq, npair * LANE, tq), lambda bi, p, qi: (bi, 0, p, 0))],
        out_specs=pl.BlockSpec((1, tq, npair * LANE), lambda bi, p, qi: (bi, qi, p)),
        out_shape=jax.ShapeDtypeStruct((b, s, pairs * LANE), _BF16),
        scratch_shapes=([pltpu.VMEM((tq, tq), _F32)] * (2 * npair)
                        + [pltpu.VMEM((LANE, tq), _F32)] * (2 * npair)),
        compiler_params=_cparams(3),
        name="mla_attention",
    )(qt, k, vt4)


def _diff_attn_kernel(qt_ref, k_ref, vt_ref, lam_ref, subln_ref, o_ref, *s_scrs, tq, lambda_init):
    qi = pl.program_id(2)
    npair = DIFF_PAIRS_PER_STEP
    lanes = [slice(pr * LANE, (pr + 1) * LANE) for pr in range(npair)]
    outs = _flash_maps([qt_ref[0, lanes[pr], :] for pr in range(npair) for _ in range(4)],
                       [lambda j, v=v, pr=pr: _key_block(k_ref, (v, 0), j, tq, lanes[pr])
                        for pr in range(npair) for v in range(4)],
                       [lambda j, pr=pr: vt_ref[0, j, lanes[pr], :] for pr in range(npair) for _ in range(4)],
                       qi, tq, s_scrs)
    lf = lam_ref[...]
    lam = (jnp.exp(jnp.sum(lf[0:1] * lf[1:2], axis=-1, keepdims=True))
           - jnp.exp(jnp.sum(lf[2:3] * lf[3:4], axis=-1, keepdims=True)) + lambda_init)
    half = 2 * DIFF_HEAD_DIM
    lane = lax.broadcasted_iota(jnp.int32, (tq, LANE), 1)
    first = lane < half
    for pr in range(npair):
        o = outs[4 * pr:4 * pr + 4]
        dt = jnp.concatenate([(o[0] - lam * o[1])[:half], (o[2] - lam * o[3])[half:]], axis=0)
        d = dt.T
        sq = d * d
        ss_a = jnp.sum(jnp.where(first, sq, 0.0), axis=-1, keepdims=True)
        ss_b = jnp.sum(jnp.where(first, 0.0, sq), axis=-1, keepdims=True)
        ms = jnp.where(first, ss_a, ss_b) * (1.0 / half)
        o_ref[0, :, lanes[pr]] = (d * lax.rsqrt(ms + SUBLN_EPS) * subln_ref[...]
                                  * (1.0 - lambda_init)).astype(o_ref.dtype)


def _diff_attention(qt, kvar, vt, lam, subln2, b, tq, lambda_init):
    s = kvar.shape[2]
    nq = s // tq
    npair = DIFF_PAIRS_PER_STEP
    pairs = DIFF_HEADS // 2
    assert pairs % npair == 0
    vt4 = vt.reshape(b, nq, vt.shape[1], tq)
    return pl.pallas_call(
        functools.partial(_diff_attn_kernel, tq=tq, lambda_init=lambda_init),
        grid=(b, pairs // npair, nq),
        in_specs=[pl.BlockSpec((1, npair * LANE, tq), lambda bi, p, qi: (bi * nq + qi, p, 0)),
                  pl.BlockSpec((4, 1, s, npair * LANE), lambda bi, p, qi: (0, bi, 0, p)),
                  pl.BlockSpec((1, nq, npair * LANE, tq), lambda bi, p, qi: (bi, 0, p, 0)),
                  _const_spec(lam.shape), _const_spec(subln2.shape)],
        out_specs=pl.BlockSpec((1, tq, npair * LANE), lambda bi, p, qi: (bi, qi, p)),
        out_shape=jax.ShapeDtypeStruct((b, s, pairs * LANE), _BF16),
        scratch_shapes=([pltpu.VMEM((tq, tq), _F32)] * (4 * npair)
                        + [pltpu.VMEM((LANE, tq), _F32)] * (4 * npair)),
        compiler_params=_cparams(3),
        name="diff_attention",
    )(qt, kvar, vt4, lam, subln2)


def _ssm_pack_kernel(x_ref, o_ref, *, ct, lane_tiles):
    L, P = SSM_CHUNK, SSM_GROUP
    per_lane_tile = LANE // P
    for s_in in range(L):
        for lt in range(lane_tiles):
            xs = x_ref[pl.ds(s_in * lane_tiles + lt, ct, stride=L * lane_tiles), :]
            xt = xs.T
            for gi in range(per_lane_tile):
                o_ref[lt * per_lane_tile + gi, s_in * P:(s_in + 1) * P, :] = (
                    xt[gi * P:(gi + 1) * P, :].astype(o_ref.dtype))


def _ssm_unpack_kernel(y_ref, o_ref, *, ct, lane_tiles):
    L, P = SSM_CHUNK, SSM_GROUP
    per_lane_tile = LANE // P
    for t_in in range(L):
        for lt in range(lane_tiles):
            z = jnp.concatenate([y_ref[lt * per_lane_tile + gi, t_in * P:(t_in + 1) * P, :].astype(_F32)
                                 for gi in range(per_lane_tile)], axis=0)
            o_ref[pl.ds(t_in * lane_tiles + lt, ct, stride=L * lane_tiles), :] = z.T


def _ssm_tiles(t, batch):
    chunks = t // batch // SSM_CHUNK
    ct = min(LANE, chunks)
    assert chunks % ct == 0
    return chunks, ct


def _ssm_pack(us, batch, groups):
    lane_tiles = groups * SSM_GROUP // LANE
    t = us.shape[0] // lane_tiles
    chunks, ct = _ssm_tiles(t, batch)
    return pl.pallas_call(
        functools.partial(_ssm_pack_kernel, ct=ct, lane_tiles=lane_tiles),
        grid=(t // (ct * SSM_CHUNK),),
        in_specs=[pl.BlockSpec((ct * SSM_CHUNK * lane_tiles, LANE), lambda i: (i, 0))],
        out_specs=pl.BlockSpec((groups, SSM_CHUNK * SSM_GROUP, ct), lambda i: (0, 0, i)),
        out_shape=jax.ShapeDtypeStruct((groups, SSM_CHUNK * SSM_GROUP, batch * chunks), _BF16),
        compiler_params=_cparams(1),
        name="ssm_pack",
    )(us)


def _ssm_unpack(yt, batch):
    groups, w, r = yt.shape
    t = r * SSM_CHUNK
    chunks, ct = _ssm_tiles(t, batch)
    lane_tiles = groups * SSM_GROUP // LANE
    return pl.pallas_call(
        functools.partial(_ssm_unpack_kernel, ct=ct, lane_tiles=lane_tiles),
        grid=(r // ct,),
        in_specs=[pl.BlockSpec((groups, w, ct), lambda i: (0, 0, i))],
        out_specs=pl.BlockSpec((ct * SSM_CHUNK * lane_tiles, LANE), lambda i: (i, 0)),
        out_shape=jax.ShapeDtypeStruct((t * lane_tiles, LANE), _F32),
        compiler_params=_cparams(1),
        name="ssm_unpack",
    )(yt)


def _ssm_kernel(u_ref, t_ref, b_ref, c_ref, ar_ref, ai_ref, d_ref, y_ref, x_scr, xs_scr, h_scr, *, batch, chunks):
    ut = u_ref[0]
    xt = _dot(b_ref[0], ut)
    x_scr[...] = xt.T
    xs_scr[...] = jnp.concatenate([xt[SSM_STATE:], xt[:SSM_STATE]], axis=0).T
    ar, ai = ar_ref[0], ai_ref[0]
    ai_s = pltpu.roll(ai, SSM_STATE, 1)

    def step(c, carry):
        h, hs = carry
        rows = pl.ds(c, batch, stride=chunks)
        h_scr[rows, :] = h
        return ar * h + ai * hs + x_scr[rows, :], ar * hs + ai_s * h + xs_scr[rows, :]

    zero = jnp.zeros((batch, 2 * SSM_STATE), _F32)
    lax.fori_loop(0, chunks, step, (zero, zero), unroll=SCAN_UNROLL)
    ht = h_scr[...].T.astype(_BF16)
    yt = _dot(t_ref[0], ut) + _dot(c_ref[0], ht) + ut.astype(_F32) * d_ref[0]
    y_ref[0] = jax.nn.gelu(yt).astype(y_ref.dtype)


def _ssm(u_t, tmat_t, bmat_t, cmat_t, ar, ai, d_col, batch):
    g, w, r = u_t.shape
    per_g = lambda shape: pl.BlockSpec((1,) + shape, lambda i: (i, 0, 0))
    return pl.pallas_call(
        functools.partial(_ssm_kernel, batch=batch, chunks=r // batch),
        grid=(g,),
        in_specs=[per_g((w, r)), per_g((w, w)), per_g((2 * SSM_STATE, w)), per_g((w, 2 * SSM_STATE)),
                  per_g((1, 2 * SSM_STATE)), per_g((1, 2 * SSM_STATE)), per_g((w, 1))],
        out_specs=per_g((w, r)),
        out_shape=jax.ShapeDtypeStruct((g, w, r), _BF16),
        scratch_shapes=[pltpu.VMEM((r, 2 * SSM_STATE), _F32)] * 3,
        compiler_params=_cparams(1),
        name="ssm_scan",
    )(u_t, tmat_t, bmat_t, cmat_t, ar, ai, d_col)


def _ssm_matrices(a_re, a_im, log_dt, b_re, b_im, c_re, c_im, d):
    L, P, N = SSM_CHUNK, SSM_GROUP, SSM_STATE
    G = a_re.shape[0]
    A = lax.complex(a_re.astype(_F32), a_im.astype(_F32))
    dt = jnp.exp(log_dt.astype(_F32))[:, None]
    a_bar = jnp.exp(A * dt)
    b_bar = ((a_bar - 1.0) / A)[..., None] * lax.complex(b_re.astype(_F32), b_im.astype(_F32))
    C = lax.complex(c_re.astype(_F32), c_im.astype(_F32))
    steps = jnp.arange(L + 1, dtype=_F32)
    apow = jnp.exp((A * dt)[None] * steps[:, None, None])
    m = jnp.real(jnp.einsum('gpn,dgn,gnq->dgpq', C, apow[:L], b_bar))
    s_idx = jnp.arange(L)[:, None]
    t_idx = jnp.arange(L)[None, :]
    lag = jnp.clip(t_idx - s_idx, 0, L - 1)
    tfull = jnp.where((t_idx >= s_idx)[:, :, None, None, None], m[lag], 0.0)
    tmat = tfull.transpose(2, 0, 4, 1, 3).reshape(G, L * P, L * P)
    bm = apow[L - 1 - jnp.arange(L)][:, :, :, None] * b_bar[None]
    bm = bm.transpose(1, 0, 3, 2).reshape(G, L * P, N)
    bmat = jnp.concatenate([jnp.real(bm), jnp.imag(bm)], axis=-1)
    cm = C[None] * apow[1:L + 1][:, :, None, :]
    cm = cm.transpose(1, 3, 0, 2).reshape(G, N, L * P)
    cmat = jnp.concatenate([jnp.real(cm), -jnp.imag(cm)], axis=1)
    al = apow[L]
    ar = jnp.concatenate([jnp.real(al), jnp.real(al)], axis=-1)[:, None, :]
    ai = jnp.concatenate([-jnp.imag(al), jnp.imag(al)], axis=-1)[:, None, :]
    dflat = jnp.tile(d.astype(_F32).reshape(G, 1, P), (1, L, 1)).reshape(G, 1, L * P)
    tr = lambda a: a.transpose(0, 2, 1).astype(_BF16)
    return tr(tmat), tr(bmat), tr(cmat), ar, ai, dflat.reshape(G, L * P, 1)


def _layernorm(z, g, b):
    mu = jnp.mean(z, axis=-1, keepdims=True)
    zc = z - mu
    var = jnp.mean(zc * zc, axis=-1, keepdims=True)
    return zc * lax.rsqrt(var + LN_EPS) * g + b


def _merge_kernel(om_ref, od_ref, ys_ref, g_ref, x_ref, wmo_ref, wdo_ref, wglu_ref, wo_ref, lng_ref, lnb_ref,
                  rw_ref, rb_ref, x1_ref, idx_ref, wts_ref, cnt_ref, *, alpha, d_model):
    def branch_dots(rows):
        n_rows = rows.stop - rows.start
        lt = wglu_ref.shape[0] // LANE
        ys = jnp.concatenate([ys_ref[pl.ds(rows.start * lt + c, n_rows, stride=lt), :] for c in range(lt)], axis=1)
        return (_dot(om_ref[rows, :], wmo_ref[...]), _dot(od_ref[rows, :], wdo_ref[...]),
                _dot(ys.astype(_BF16), wglu_ref[...]))

    def gate_merge(rows, ya, yb, gl):
        yc = gl[:, :d_model] * jax.nn.sigmoid(gl[:, d_model:])
        merged = (g_ref[rows, 0:d_model].astype(_F32) * ya + g_ref[rows, d_model:2 * d_model].astype(_F32) * yb
                  + g_ref[rows, 2 * d_model:3 * d_model].astype(_F32) * yc)
        return merged.astype(_BF16)

    def out_norm(rows, merged):
        z = alpha * x_ref[rows, :] + _dot(merged, wo_ref[...])
        x1 = _layernorm(z, lng_ref[...], lnb_ref[...])
        for c in range(d_model // LANE):
            x1_ref[pl.ds(rows.start * TOKEN_TILE + c, rows.stop - rows.start, stride=TOKEN_TILE), :] = (
                x1[:, c * LANE:(c + 1) * LANE])
        return x1

    def router_logits(x1):
        return _dot_nt(rw_ref[...], x1, precision=lax.Precision.HIGHEST)

    def route(rows, logits):
        scores = jax.nn.sigmoid(logits)
        biased = scores + rb_ref[...]
        epg = logits.shape[0] // N_GROUPS
        a = [biased[j * N_GROUPS:(j + 1) * N_GROUPS] for j in range(epg)]
        sc = [scores[j * N_GROUPS:(j + 1) * N_GROUPS] for j in range(epg)]
        gs = None
        for i in range(epg):
            for j in range(i + 1, epg):
                pair = a[i] + a[j]
                gs = pair if gs is None else jnp.maximum(gs, pair)
        giota = lax.broadcasted_iota(jnp.int32, gs.shape, 0)
        gmax = jnp.max(gs, axis=0, keepdims=True)
        gsel = jnp.min(jnp.where(gs == gmax, giota, N_GROUPS), axis=0, keepdims=True)
        hot = giota == gsel
        val = [jnp.sum(jnp.where(hot, a[j], 0.0), axis=0, keepdims=True) for j in range(epg)]
        raw = [jnp.sum(jnp.where(hot, sc[j], 0.0), axis=0, keepdims=True) for j in range(epg)]
        b1, i1, w1 = val[0], jnp.zeros_like(gsel), raw[0]
        for j in range(1, epg):
            take = val[j] > b1
            b1 = jnp.where(take, val[j], b1)
            i1 = jnp.where(take, j, i1)
            w1 = jnp.where(take, raw[j], w1)
        b2 = jnp.full_like(b1, -jnp.inf)
        i2 = jnp.zeros_like(gsel)
        w2 = jnp.zeros_like(w1)
        for j in range(epg):
            take = (i1 != j) & (val[j] > b2)
            b2 = jnp.where(take, val[j], b2)
            i2 = jnp.where(take, j, i2)
            w2 = jnp.where(take, raw[j], w2)
        wsum = w1 + w2
        e1, e2 = gsel * epg + i1, gsel * epg + i2
        idx_ref[:, rows] = jnp.concatenate([e1, e2], axis=0)
        wts_ref[:, rows] = jnp.concatenate([w1 / wsum, w2 / wsum], axis=0)
        eiota = lax.broadcasted_iota(jnp.int32, logits.shape, 0)
        hit = (eiota == e1).astype(jnp.int32) + (eiota == e2).astype(jnp.int32)
        part = hit[:, 0:LANE]
        for c in range(1, hit.shape[1] // LANE):
            part = part + hit[:, c * LANE:(c + 1) * LANE]
        return part

    tm = x_ref.shape[0]
    sub = tm // MERGE_SUBTILES
    rows = [slice(k * sub, (k + 1) * sub) for k in range(MERGE_SUBTILES)]
    dots = [branch_dots(r) for r in rows]
    merged = [gate_merge(r, *d) for r, d in zip(rows, dots)]
    x1s = [out_norm(r, m) for r, m in zip(rows, merged)]
    logits = [router_logits(x1) for x1 in x1s]
    parts = [route(r, lg) for r, lg in zip(rows, logits)]

    @pl.when(pl.program_id(0) == 0)
    def _():
        cnt_ref[...] = jnp.zeros_like(cnt_ref)

    cnt_ref[...] += functools.reduce(lambda p, q: p + q, parts)


def _merge(om, od, ys, gates, x2d, wmo, wdo, wglu, wo, lng, lnb, rw, rb, tm, alpha):
    t, d_model = x2d.shape
    n_experts = rw.shape[0]
    assert d_model == TOKEN_TILE * LANE and tm % MERGE_SUBTILES == 0
    row = lambda w: pl.BlockSpec((tm, w), lambda i: (i, 0))
    col = pl.BlockSpec((TOP_K, tm), lambda i: (0, i))
    return pl.pallas_call(
        functools.partial(_merge_kernel, alpha=alpha, d_model=d_model),
        grid=(t // tm,),
        in_specs=[row(om.shape[1]), row(od.shape[1]),
                  pl.BlockSpec((tm * (ys.shape[0] // t), LANE), lambda i: (i, 0)), row(gates.shape[1]), row(d_model),
                  _const_spec(wmo.shape), _const_spec(wdo.shape), _const_spec(wglu.shape), _const_spec(wo.shape),
                  _const_spec(lng.shape), _const_spec(lnb.shape), _const_spec(rw.shape), _const_spec(rb.shape)],
        out_specs=(pl.BlockSpec((tm * TOKEN_TILE, LANE), lambda i: (i, 0)), col, col,
                   pl.BlockSpec((n_experts, LANE), lambda i: (0, 0))),
        out_shape=(jax.ShapeDtypeStruct((t * TOKEN_TILE, LANE), _F32), jax.ShapeDtypeStruct((TOP_K, t), jnp.int32),
                   jax.ShapeDtypeStruct((TOP_K, t), _F32), jax.ShapeDtypeStruct((n_experts, LANE), jnp.int32)),
        compiler_params=_cparams(1),
        name="merge_router",
    )(om, od, ys, gates, x2d, wmo, wdo, wglu, wo, lng, lnb, rw, rb)


DMA_UNROLL = 16


def _moe_kernel(be_ref, cnt_ref, nu_ref, tok_ref, tokn_ref, dst_ref, x_hbm, wg_ref, wu_ref, wd_ref, out_hbm,
                xbuf, ybuf, wgb, wub, wdb, sem_in, sem_out, *, mb):
    i = pl.program_id(0)
    n_used = nu_ref[0]
    slot = lax.rem(i, 2)

    def for_rows(start_row):
        def body(j, carry):
            for u in range(DMA_UNROLL):
                start_row(j * DMA_UNROLL + u, u % 2)
            return carry
        lax.fori_loop(0, mb // DMA_UNROLL, body, 0)

    def for_first_rows(n, fn):
        def body(r, carry):
            fn(r)
            return carry
        lax.fori_loop(0, n, body, 0)

    def tile_rows(tok):
        return pl.ds(pl.multiple_of(tok * TOKEN_TILE, TOKEN_TILE), TOKEN_TILE)

    def gather_row(t_ref, s, r):
        return pltpu.make_async_copy(x_hbm.at[tile_rows(t_ref[0, 0, r]), :], xbuf.at[s, tile_rows(r), :],
                                     sem_in.at[s])

    def scatter_row(s, r, dst):
        return pltpu.make_async_copy(ybuf.at[s, tile_rows(r), :], out_hbm.at[tile_rows(dst), :], sem_out.at[s])

    def issue_gather(t_ref, s):
        for_rows(lambda r, pr: gather_row(t_ref, s, r).start(priority=pr))

    def wait_gather(s):
        pltpu.make_async_copy(x_hbm.at[pl.ds(0, mb * TOKEN_TILE), :], xbuf.at[s], sem_in.at[s]).wait()

    def issue_scatter(s, cnt):
        @pl.when(cnt == mb)
        def _():
            for_rows(lambda r, pr: scatter_row(s, r, dst_ref[0, 0, r]).start(priority=pr))

        @pl.when(cnt < mb)
        def _():
            for_first_rows(cnt, lambda r: scatter_row(s, r, dst_ref[0, 0, r]).start())

    def wait_scatter(s, cnt):
        @pl.when(cnt == mb)
        def _():
            pltpu.make_async_copy(ybuf.at[s], out_hbm.at[pl.ds(0, mb * TOKEN_TILE), :], sem_out.at[s]).wait()

        @pl.when(cnt < mb)
        def _():
            for_first_rows(cnt, lambda r: scatter_row(s, r, 0).wait())

    @pl.when(i < n_used)
    def _():
        @pl.when(i == 0)
        def _():
            issue_gather(tok_ref, 0)

        @pl.when(i + 1 < n_used)
        def _():
            issue_gather(tokn_ref, 1 - slot)

        @pl.when((i == 0) | (be_ref[i] != be_ref[jnp.maximum(i - 1, 0)]))
        def _():
            wgb[...] = wg_ref[0, 0].astype(_BF16)
            wub[...] = wu_ref[0, 0].astype(_BF16)
            wdb[...] = wd_ref[0, 0].astype(_BF16)

        wait_gather(slot)

        @pl.when(i >= 2)
        def _():
            wait_scatter(slot, cnt_ref[jnp.maximum(i - 2, 0)])

        xb = jnp.concatenate([xbuf[slot, pl.ds(c, mb, stride=TOKEN_TILE), :] for c in range(TOKEN_TILE)],
                             axis=1).astype(_BF16)
        h = jax.nn.silu(_dot(xb, wgb[...])) * _dot(xb, wub[...])
        y = _dot(h.astype(_BF16), wdb[...])
        for c in range(TOKEN_TILE):
            ybuf[slot, pl.ds(c, mb, stride=TOKEN_TILE), :] = y[:, c * LANE:(c + 1) * LANE]
        issue_scatter(slot, cnt_ref[i])

        @pl.when(i == n_used - 1)
        def _():
            @pl.when(i >= 1)
            def _():
                wait_scatter(1 - slot, cnt_ref[jnp.maximum(i - 1, 0)])
            wait_scatter(slot, cnt_ref[i])


def _moe(block_e, block_cnt, n_used, row_tok, row_dst, x1t, wg, wu, wd, layer):
    t = x1t.shape[0] // TOKEN_TILE
    d_model = TOKEN_TILE * LANE
    n_blocks, mb = row_tok.shape
    ff = wg.shape[-1]
    assert mb % DMA_UNROLL == 0
    smem_row = lambda f: pl.BlockSpec((1, 1, mb), lambda i, be, cnt, nu: (f(i), 0, 0), memory_space=pltpu.SMEM)
    wspec = lambda a, b: pl.BlockSpec((1, 1, a, b), lambda i, be, cnt, nu: (layer, be[i], 0, 0))
    grid_spec = pltpu.PrefetchScalarGridSpec(
        num_scalar_prefetch=3,
        grid=(n_blocks,),
        in_specs=[smem_row(lambda i: i), smem_row(lambda i: jnp.minimum(i + 1, n_blocks - 1)), smem_row(lambda i: i),
                  pl.BlockSpec(memory_space=pl.ANY),
                  wspec(d_model, ff), wspec(d_model, ff), wspec(ff, d_model)],
        out_specs=pl.BlockSpec(memory_space=pl.ANY),
        scratch_shapes=[pltpu.VMEM((2, mb * TOKEN_TILE, LANE), _F32), pltpu.VMEM((2, mb * TOKEN_TILE, LANE), _F32),
                        pltpu.VMEM((d_model, ff), _BF16), pltpu.VMEM((d_model, ff), _BF16),
                        pltpu.VMEM((ff, d_model), _BF16),
                        pltpu.SemaphoreType.DMA((2,)), pltpu.SemaphoreType.DMA((2,))])
    tok3 = row_tok.reshape(n_blocks, 1, mb)
    return pl.pallas_call(
        functools.partial(_moe_kernel, mb=mb),
        grid_spec=grid_spec,
        out_shape=jax.ShapeDtypeStruct((TOP_K * t * TOKEN_TILE, LANE), _F32),
        compiler_params=_cparams(1),
        name="moe_experts",
    )(block_e, block_cnt, n_used, tok3, tok3, row_dst.reshape(n_blocks, 1, mb), x1t, wg, wu, wd)


def _moe_plan(idx, counts, mb):
    k, t = idx.shape
    a_total = k * t
    n_experts = counts.shape[0]
    e_flat = idx.reshape(a_total)
    order = jnp.argsort(e_flat).astype(jnp.int32)
    starts = jnp.cumsum(counts) - counts
    padded = (counts + mb - 1) // mb * mb
    pad_ends = jnp.cumsum(padded)
    pad_starts = pad_ends - padded
    n_blocks = -(-a_total // mb) + n_experts
    blk = jnp.arange(n_blocks, dtype=jnp.int32)
    blk_start = blk * mb
    block_e = jnp.minimum(jnp.sum((pad_ends[None, :] <= blk_start[:, None]).astype(jnp.int32), axis=1),
                          n_experts - 1)
    off0 = blk_start - pad_starts[block_e]
    cnt = jnp.clip(counts[block_e] - off0, 0, mb)
    n_used = jnp.sum((cnt > 0).astype(jnp.int32)).reshape(1)
    lane = jnp.arange(mb, dtype=jnp.int32)[None, :]
    valid = lane < cnt[:, None]
    src = jnp.clip((starts[block_e] + off0)[:, None] + lane, 0, a_total - 1)
    a_r = order[src]
    row_dst = jnp.where(valid, a_r, 0).astype(jnp.int32)
    row_tok = jnp.where(valid, a_r % t, 0).astype(jnp.int32)
    return block_e.astype(jnp.int32), cnt.astype(jnp.int32), n_used, row_tok, row_dst


def _final_kernel(x_ref, y0_ref, y1_ref, w_ref, lng_ref, lnb_ref, o_ref, *, alpha):
    tm = o_ref.shape[0]

    def tokens(ref):
        return jnp.concatenate([ref[pl.ds(c, tm, stride=TOKEN_TILE), :] for c in range(TOKEN_TILE)], axis=1)

    w = w_ref[...]
    z = alpha * tokens(x_ref) + (tokens(y0_ref) * w[:, 0:1] + tokens(y1_ref) * w[:, 1:2])
    o_ref[...] = _layernorm(z, lng_ref[...], lnb_ref[...])


def _final(x1t, yslots, wts_t, lng, lnb, tm, alpha):
    t = x1t.shape[0] // TOKEN_TILE
    d_model = TOKEN_TILE * LANE
    nt = t // tm
    tile = lambda f: pl.BlockSpec((tm * TOKEN_TILE, LANE), lambda i: (f(i), 0))
    return pl.pallas_call(
        functools.partial(_final_kernel, alpha=alpha),
        grid=(nt,),
        in_specs=[tile(lambda i: i), tile(lambda i: i), tile(lambda i: nt + i),
                  pl.BlockSpec((tm, TOP_K), lambda i: (i, 0)),
                  _const_spec(lng.shape), _const_spec(lnb.shape)],
        out_specs=pl.BlockSpec((tm, d_model), lambda i: (i, 0)),
        out_shape=jax.ShapeDtypeStruct((t, d_model), _F32),
        compiler_params=_cparams(1),
        name="moe_residual_ln",
    )(x1t, yslots, yslots, wts_t, lng, lnb)


def _prep_w_in(w):
    d_model = w.shape[0]
    seg = DIFF_HEADS * 2 * DIFF_HEAD_DIM
    o = 0
    cq = w[:, o:o + MLA_Q_LORA]; o += MLA_Q_LORA
    ckv = w[:, o:o + MLA_KV_LORA]; o += MLA_KV_LORA
    kr = w[:, o:o + MLA_ROPE]; o += MLA_ROPE
    rest = w[:, o:]
    z = lambda n: jnp.zeros((d_model, n), w.dtype)
    kr_chunk = jnp.concatenate([z(MLA_NOPE), kr, z(LANE - MLA_NOPE - MLA_ROPE)], axis=1)
    assert MLA_Q_LORA + MLA_KV_LORA + LANE == 4 * LANE and rest.shape[1] == 4 * seg + N_BRANCH * d_model
    return jnp.concatenate([cq, ckv, kr_chunk, rest], axis=1).astype(_BF16)


def _prep_w_uq(w):
    dq = MLA_NOPE + MLA_ROPE
    w3 = w.reshape(w.shape[0], MLA_HEADS, dq)
    w3 = jnp.pad(w3, ((0, 0), (0, 0), (0, LANE - dq)))
    return w3.reshape(w.shape[0], MLA_HEADS * LANE).astype(_BF16)


def _prep_w_ukv(w):
    w3 = w.reshape(w.shape[0], MLA_HEADS, MLA_NOPE + MLA_V)
    wk = jnp.pad(w3[:, :, :MLA_NOPE], ((0, 0), (0, 0), (0, LANE - MLA_NOPE)))
    wv = w3[:, :, MLA_NOPE:]
    return (wk.reshape(w.shape[0], MLA_HEADS * LANE).astype(_BF16),
            wv.reshape(w.shape[0], MLA_HEADS * MLA_V).astype(_BF16))


def kernel(x, positions, w_in, b_gate, mla_q_norm, mla_w_uq, mla_kv_norm, mla_w_ukv, mla_w_out,
           diff_lambda, diff_subln, diff_w_out, ssm_a_re, ssm_a_im, ssm_log_dt, ssm_b_re, ssm_b_im,
           ssm_c_re, ssm_c_im, ssm_d, ssm_w_glu, w_o, ln_gain, ln_bias, router_w, router_bias,
           moe_w_gate, moe_w_up, moe_w_down):
    B, S, D = x.shape
    T = B * S
    depth = w_in.shape[0]
    n_experts = router_w.shape[1]
    epg = n_experts // N_GROUPS
    alpha = float((2 * depth) ** 0.25)
    tm = min(ROW_TILE, T)
    tq = min(ATTN_TILE, S)
    L, P = SSM_CHUNK, SSM_GROUP
    G = ssm_a_re.shape[1]
    assert T % tm == 0 and S % tq == 0 and S % L == 0 and (TOP_K * T) % MOE_ROWS == 0
    assert MLA_HEADS * MLA_V == DIFF_HEADS * 2 * DIFF_HEAD_DIM == G * P

    (inv_a, sel_a), (inv_d, sel_d) = _rope_lane_patterns()
    pos_col = positions.reshape(T, 1).astype(jnp.int32)
    taba = _rope_tables(pos_col, inv_a, sel_a, tm)
    tabd = _rope_tables(pos_col, inv_d, sel_d, tm)

    rw = router_w.T.reshape(N_GROUPS, epg, D).transpose(1, 0, 2).reshape(n_experts, D).astype(_F32)
    rb = router_bias.reshape(N_GROUPS, epg).T.reshape(n_experts, 1).astype(_F32)

    x2d = x.reshape(T, D)
    for l in range(depth):
        lambda_init = 0.8 - 0.6 * math.exp(-0.3 * l)
        w1 = _prep_w_in(w_in[l])
        wuq = _prep_w_uq(mla_w_uq[l])
        wk, wv = _prep_w_ukv(mla_w_ukv[l])
        qmt, km, vmt, qdt, kdv, vdt, us, gates = _inproj(
            x2d, taba, tabd, w1, mla_q_norm[l][None].astype(_F32), wuq, mla_kv_norm[l][None].astype(_F32),
            wk, wv, b_gate[l].reshape(1, N_BRANCH * D).astype(_F32), tq)

        o_mla = _mla_attention(qmt, km.reshape(B, S, -1), vmt, B, tq)
        subln2 = jnp.tile(diff_subln[l].astype(_F32), 2)[None]
        o_diff = _diff_attention(qdt, kdv.reshape(4, B, S, -1), vdt, diff_lambda[l].astype(_F32), subln2,
                                 B, tq, lambda_init)

        mats = _ssm_matrices(ssm_a_re[l], ssm_a_im[l], ssm_log_dt[l], ssm_b_re[l], ssm_b_im[l],
                             ssm_c_re[l], ssm_c_im[l], ssm_d[l])
        ys = _ssm_unpack(_ssm(_ssm_pack(us, B, G), *mats, batch=B), B)

        x1, idx, wts, cnt_part = _merge(
            o_mla.reshape(T, -1), o_diff.reshape(T, -1), ys, gates, x2d,
            mla_w_out[l].astype(_BF16), diff_w_out[l].astype(_BF16), ssm_w_glu[l].astype(_BF16),
            w_o[l].astype(_BF16), ln_gain[l, 0][None].astype(_F32), ln_bias[l, 0][None].astype(_F32),
            rw, rb, min(MERGE_TILE, T), alpha)

        plan = _moe_plan(idx, jnp.sum(cnt_part, axis=1), MOE_ROWS)
        yslots = _moe(*plan, x1, moe_w_gate.astype(_F32), moe_w_up.astype(_F32), moe_w_down.astype(_F32), l)
        x2d = _final(x1, yslots, wts.T, ln_gain[l, 1][None].astype(_F32), ln_bias[l, 1][None].astype(_F32),
                     tm, alpha)
    return x2d.reshape(B, S, D)
```

```python
import functools
import math

import jax
import jax.numpy as jnp
from jax import lax
from jax.experimental import pallas as pl
from jax.experimental.pallas import tpu as pltpu

MLA_HEADS = 8
MLA_Q_LORA = 256
MLA_KV_LORA = 128
MLA_NOPE = 64
MLA_ROPE = 32
MLA_V = 64
DIFF_HEADS = 8
DIFF_HEAD_DIM = 32
DIFF_ROT = DIFF_HEAD_DIM // 4
SSM_GROUP = 16
SSM_STATE = 64
N_BRANCH = 3
ROPE_THETA = 500000.0
N_GROUPS = 8
TOP_K = 2
LN_EPS = 1e-5
RMS_EPS = 1e-6
SUBLN_EPS = 1e-5

LANE = 128
ROW_TILE = 512
ATTN_TILE = 512
MLA_PAIRS_PER_STEP = 4
DIFF_PAIRS_PER_STEP = 2
SSM_CHUNK = 16
SCAN_UNROLL = 4
MOE_ROWS = 256
MERGE_TILE = 1024
MERGE_SUBTILES = 4
TOKEN_TILE = 8
VMEM_LIMIT = 56 * 1024 * 1024

NEG_BIG = -1e30
LOG2E = math.log2(math.e)

_F32 = jnp.float32
_BF16 = jnp.bfloat16


def _cparams(n_axes):
    return pltpu.CompilerParams(dimension_semantics=("arbitrary",) * n_axes,
                                vmem_limit_bytes=VMEM_LIMIT)


def _dot(a, b):
    return jnp.dot(a, b, preferred_element_type=_F32)


def _dot_nt(a, b, precision=None):
    return lax.dot_general(a, b, (((1,), (1,)), ((), ())), precision=precision,
                           preferred_element_type=_F32)


def _const_spec(shape):
    zeros = (0,) * len(shape)
    return pl.BlockSpec(shape, lambda *_: zeros, pipeline_mode=pl.Buffered(1))


def _rope_table_kernel(pos_ref, inv_ref, sel_ref, out_ref):
    pos = pos_ref[...].astype(_F32)
    ang = pos * inv_ref[...]
    c, s = jnp.cos(ang), jnp.sin(ang)
    sel = sel_ref[...]
    out_ref[0] = c * (sel[0:1] + sel[1:2]) + sel[2:3]
    out_ref[1] = -s * sel[0:1]
    out_ref[2] = s * sel[1:2]


def _rope_tables(pos_col, inv_row, sel_rows, tm):
    t = pos_col.shape[0]
    return pl.pallas_call(
        _rope_table_kernel,
        grid=(t // tm,),
        in_specs=[pl.BlockSpec((tm, 1), lambda i: (i, 0)), _const_spec((1, LANE)), _const_spec((3, LANE))],
        out_specs=pl.BlockSpec((3, tm, LANE), lambda i: (0, i, 0)),
        out_shape=jax.ShapeDtypeStruct((3, t, LANE), _F32),
        compiler_params=_cparams(1),
        name="rope_tables",
    )(pos_col, inv_row, sel_rows)


def _rope_lane_patterns():
    lane = jnp.arange(LANE)
    half_a = MLA_ROPE // 2
    in_rope = (lane >= MLA_NOPE) & (lane < MLA_NOPE + MLA_ROPE)
    fi = (lane - MLA_NOPE) % half_a
    inv_a = jnp.where(in_rope, ROPE_THETA ** (-(2.0 * fi.astype(_F32)) / MLA_ROPE), 0.0)
    x1_a = in_rope & (lane < MLA_NOPE + half_a)
    x2_a = in_rope & ~x1_a
    sel_a = jnp.stack([x1_a, x2_a, ~in_rope]).astype(_F32)
    half_d = DIFF_ROT // 2
    d = lane % DIFF_HEAD_DIM
    in_rot = d < DIFF_ROT
    inv_d = jnp.where(in_rot, ROPE_THETA ** (-(2.0 * (d % half_d).astype(_F32)) / DIFF_ROT), 0.0)
    x1_d = d < half_d
    x2_d = in_rot & ~x1_d
    sel_d = jnp.stack([x1_d, x2_d, ~in_rot]).astype(_F32)
    return (inv_a[None].astype(_F32), sel_a), (inv_d[None].astype(_F32), sel_d)


def _rope_chunk(x, tab_ref, half):
    return (x * tab_ref[0] + pltpu.roll(x, LANE - half, 1) * tab_ref[1]
            + pltpu.roll(x, half, 1) * tab_ref[2])


def _rms(x, g, eps):
    return x * lax.rsqrt(jnp.mean(x * x, axis=-1, keepdims=True) + eps) * g


def _inproj_kernel(x_ref, taba_ref, tabd_ref, w1_ref, qn_ref, wuq_ref, kvn_ref, wk_ref, wv_ref, bg_ref,
                   qmt_ref, km_ref, vmt_ref, qdt_ref, kd_ref, vdt_ref, us_ref, gates_ref,
                   *, q_scale_mla, q_scale_diff, d_model):
    xb = x_ref[...].astype(_BF16)
    n_mla = MLA_HEADS
    lat = _dot(xb, w1_ref[:, 0:4 * LANE])
    cqn = _rms(lat[:, :MLA_Q_LORA], qn_ref[...], RMS_EPS).astype(_BF16)
    q = _dot(cqn, wuq_ref[...])
    for h in range(n_mla):
        sl = slice(h * LANE, (h + 1) * LANE)
        qh = _rope_chunk(q[:, sl], taba_ref, MLA_ROPE // 2) * q_scale_mla
        qmt_ref[0, sl, :] = qh.T.astype(_BF16)
    ckvn = _rms(lat[:, MLA_Q_LORA:MLA_Q_LORA + MLA_KV_LORA], kvn_ref[...], RMS_EPS).astype(_BF16)
    kn = _dot(ckvn, wk_ref[...])
    kr = _rope_chunk(lat[:, 3 * LANE:4 * LANE], taba_ref, MLA_ROPE // 2)
    for h in range(n_mla):
        sl = slice(h * LANE, (h + 1) * LANE)
        km_ref[:, sl] = (kn[:, sl] + kr).astype(_BF16)
    vm = _dot(ckvn, wv_ref[...])
    seg = DIFF_HEADS * 2 * DIFF_HEAD_DIM
    for c in range(seg // LANE):
        sl = slice(c * LANE, (c + 1) * LANE)
        vmt_ref[0, sl, :] = vm[:, sl].T.astype(_BF16)

    base = 4 * LANE
    lane = lax.broadcasted_iota(jnp.int32, (1, LANE), 1)
    comp = lane // DIFF_HEAD_DIM
    qd = _dot(xb, w1_ref[:, base:base + seg])
    kd = _dot(xb, w1_ref[:, base + seg:base + 2 * seg])
    vd = _dot(xb, w1_ref[:, base + 2 * seg:base + 3 * seg])
    for c in range(seg // LANE):
        sl = slice(c * LANE, (c + 1) * LANE)
        qc = _rope_chunk(qd[:, sl], tabd_ref, DIFF_ROT // 2) * q_scale_diff
        qdt_ref[0, sl, :] = qc.T.astype(_BF16)
        kc = _rope_chunk(kd[:, sl], tabd_ref, DIFF_ROT // 2)
        for v in range(4):
            kd_ref[v, :, sl] = jnp.where(comp == v, kc, 0.0).astype(_BF16)
        vdt_ref[0, sl, :] = vd[:, sl].T.astype(_BF16)
    us = _dot(xb, w1_ref[:, base + 3 * seg:base + 4 * seg])
    for c in range(seg // LANE):
        us_ref[pl.ds(c, us.shape[0], stride=seg // LANE), :] = us[:, c * LANE:(c + 1) * LANE]
    gbase = base + 4 * seg
    for c in range(N_BRANCH):
        sl = slice(c * d_model, (c + 1) * d_model)
        g = _dot(xb, w1_ref[:, gbase + c * d_model:gbase + (c + 1) * d_model]) + bg_ref[:, sl]
        gates_ref[:, sl] = jax.nn.sigmoid(g).astype(_BF16)


def _inproj(x2d, taba, tabd, w1, qn, wuq, kvn, wk, wv, bg, tm):
    t, d_model = x2d.shape
    nt = t // tm
    seg = DIFF_HEADS * 2 * DIFF_HEAD_DIM
    hm = MLA_HEADS * LANE
    row = lambda w: pl.BlockSpec((tm, w), lambda i: (i, 0))
    colt = lambda w: pl.BlockSpec((1, w, tm), lambda i: (i, 0, 0))
    tab = pl.BlockSpec((3, tm, LANE), lambda i: (0, i, 0))
    kern = functools.partial(
        _inproj_kernel,
        q_scale_mla=float((MLA_NOPE + MLA_ROPE) ** -0.5 * LOG2E),
        q_scale_diff=float(DIFF_HEAD_DIM ** -0.5 * LOG2E),
        d_model=d_model)
    out_shapes = (
        jax.ShapeDtypeStruct((nt, hm, tm), _BF16), jax.ShapeDtypeStruct((t, hm), _BF16),
        jax.ShapeDtypeStruct((nt, MLA_HEADS * MLA_V, tm), _BF16),
        jax.ShapeDtypeStruct((nt, seg, tm), _BF16), jax.ShapeDtypeStruct((4, t, seg), _BF16),
        jax.ShapeDtypeStruct((nt, seg, tm), _BF16), jax.ShapeDtypeStruct((t * (seg // LANE), LANE), _F32),
        jax.ShapeDtypeStruct((t, N_BRANCH * d_model), _BF16))
    out_specs = (colt(hm), row(hm), colt(MLA_HEADS * MLA_V), colt(seg),
                 pl.BlockSpec((4, tm, seg), lambda i: (0, i, 0)), colt(seg),
                 pl.BlockSpec((tm * (seg // LANE), LANE), lambda i: (i, 0)), row(N_BRANCH * d_model))
    return pl.pallas_call(
        kern,
        grid=(nt,),
        in_specs=[row(d_model), tab, tab, _const_spec(w1.shape), _const_spec(qn.shape), _const_spec(wuq.shape),
                  _const_spec(kvn.shape), _const_spec(wk.shape), _const_spec(wv.shape), _const_spec(bg.shape)],
        out_specs=out_specs,
        out_shape=out_shapes,
        compiler_params=_cparams(1),
        name="inproj",
    )(x2d, taba, tabd, w1, qn, wuq, kvn, wk, wv, bg)


def _flash_maps(q_ts, load_ks, load_vts, qi, tq, s_scrs):
    n = len(q_ts)
    a_scrs = s_scrs[n:]

    def qk(i, j):
        s_scrs[i][...] = _dot(load_ks[i](j), q_ts[i])

    def step(j, carry, diagonal):
        if diagonal:
            r = lax.broadcasted_iota(jnp.int32, (tq, tq), 0)
            c = lax.broadcasted_iota(jnp.int32, (tq, tq), 1)
            keep = r <= c
        out = []
        for i in range(n):
            if i + 1 < n:
                qk(i + 1, j)
            elif not diagonal:
                qk(0, j + 1)
            m, l = carry[i]
            s = s_scrs[i][...]
            if diagonal:
                s = jnp.where(keep, s, NEG_BIG)
            m_new = jnp.maximum(m, jnp.max(s, axis=0, keepdims=True))
            alpha = jnp.exp2(m - m_new)
            p = jnp.exp2(s - m_new)
            l = alpha * l + jnp.sum(p, axis=0, keepdims=True)
            a_scrs[i][...] = alpha * a_scrs[i][...] + _dot(load_vts[i](j), p.astype(_BF16))
            out.append((m_new, l))
        return tuple(out)

    init = tuple((jnp.full((1, tq), NEG_BIG, _F32), jnp.zeros((1, tq), _F32)) for _ in range(n))
    for i in range(n):
        a_scrs[i][...] = jnp.zeros_like(a_scrs[i])
    qk(0, 0)
    carry = lax.fori_loop(0, qi, lambda j, c: step(j, c, False), init)
    carry = step(qi, carry, True)
    return [a_scrs[i][...] / carry[i][1] for i in range(n)]


def _key_block(k_ref, lead, j, tq, lanes):
    return k_ref[lead + (pl.ds(pl.multiple_of(j * tq, tq), tq), lanes)]


def _mla_attn_kernel(qt_ref, k_ref, vt_ref, o_ref, *s_scrs, tq):
    qi = pl.program_id(2)
    npair = MLA_PAIRS_PER_STEP
    sls = [slice(hh * LANE, (hh + 1) * LANE) for hh in range(2 * npair)]
    outs = _flash_maps([qt_ref[0, sl, :] for sl in sls],
                       [lambda j, sl=sl: _key_block(k_ref, (0,), j, tq, sl) for sl in sls],
                       [lambda j, pr=pr: vt_ref[0, j, pr * LANE:(pr + 1) * LANE, :]
                        for pr in range(npair) for _ in range(2)],
                       qi, tq, s_scrs)
    for pr in range(npair):
        ot = jnp.concatenate([outs[2 * pr][:MLA_V], outs[2 * pr + 1][MLA_V:]], axis=0)
        o_ref[0, :, pr * LANE:(pr + 1) * LANE] = ot.T.astype(o_ref.dtype)


def _mla_attention(qt, k, vt, b, tq):
    s = k.shape[1]
    nq = s // tq
    npair = MLA_PAIRS_PER_STEP
    pairs = MLA_HEADS // 2
    assert pairs % npair == 0
    vt4 = vt.reshape(b, nq, vt.shape[1], tq)
    return pl.pallas_call(
        functools.partial(_mla_attn_kernel, tq=tq),
        grid=(b, pairs // npair, nq),
        in_specs=[pl.BlockSpec((1, 2 * npair * LANE, tq), lambda bi, p, qi: (bi * nq + qi, p, 0)),
                  pl.BlockSpec((1, s, 2 * npair * LANE), lambda bi, p, qi: (bi, 0, p)),
                  pl.BlockSpec((1, nq, npair * LANE, tq), lambda bi, p, qi: (bi, 0, p, 0))],
        out_specs=pl.BlockSpec((1, tq, npair * LANE), lambda bi, p, qi: (bi, qi, p)),
        out_shape=jax.ShapeDtypeStruct((b, s, pairs * LANE), _BF16),
        scratch_shapes=([pltpu.VMEM((tq, tq), _F32)] * (2 * npair)
                        + [pltpu.VMEM((LANE, tq), _F32)] * (2 * npair)),
        compiler_params=_cparams(3),
        name="mla_attention",
    )(qt, k, vt4)


def _diff_attn_kernel(qt_ref, k_ref, vt_ref, lam_ref, subln_ref, o_ref, *s_scrs, tq, lambda_init):
    qi = pl.program_id(2)
    npair = DIFF_PAIRS_PER_STEP
    lanes = [slice(pr * LANE, (pr + 1) * LANE) for pr in range(npair)]
    outs = _flash_maps([qt_ref[0, lanes[pr], :] for pr in range(npair) for _ in range(4)],
                       [lambda j, v=v, pr=pr: _key_block(k_ref, (v, 0), j, tq, lanes[pr])
                        for pr in range(npair) for v in range(4)],
                       [lambda j, pr=pr: vt_ref[0, j, lanes[pr], :] for pr in range(npair) for _ in range(4)],
                       qi, tq, s_scrs)
    lf = lam_ref[...]
    lam = (jnp.exp(jnp.sum(lf[0:1] * lf[1:2], axis=-1, keepdims=True))
           - jnp.exp(jnp.sum(lf[2:3] * lf[3:4], axis=-1, keepdims=True)) + lambda_init)
    half = 2 * DIFF_HEAD_DIM
    lane = lax.broadcasted_iota(jnp.int32, (tq, LANE), 1)
    first = lane < half
    for pr in range(npair):
        o = outs[4 * pr:4 * pr + 4]
        dt = jnp.concatenate([(o[0] - lam * o[1])[:half], (o[2] - lam * o[3])[half:]], axis=0)
        d = dt.T
        sq = d * d
        ss_a = jnp.sum(jnp.where(first, sq, 0.0), axis=-1, keepdims=True)
        ss_b = jnp.sum(jnp.where(first, 0.0, sq), axis=-1, keepdims=True)
        ms = jnp.where(first, ss_a, ss_b) * (1.0 / half)
        o_ref[0, :, lanes[pr]] = (d * lax.rsqrt(ms + SUBLN_EPS) * subln_ref[...]
                                  * (1.0 - lambda_init)).astype(o_ref.dtype)


def _diff_attention(qt, kvar, vt, lam, subln2, b, tq, lambda_init):
    s = kvar.shape[2]
    nq = s // tq
    npair = DIFF_PAIRS_PER_STEP
    pairs = DIFF_HEADS // 2
    assert pairs % npair == 0
    vt4 = vt.reshape(b, nq, vt.shape[1], tq)
    return pl.pallas_call(
        functools.partial(_diff_attn_kernel, tq=tq, lambda_init=lambda_init),
        grid=(b, pairs // npair, nq),
        in_specs=[pl.BlockSpec((1, npair * LANE, tq), lambda bi, p, qi: (bi * nq + qi, p, 0)),
                  pl.BlockSpec((4, 1, s, npair * LANE), lambda bi, p, qi: (0, bi, 0, p)),
                  pl.BlockSpec((1, nq, npair * LANE, tq), lambda bi, p, qi: (bi, 0, p, 0)),
                  _const_spec(lam.shape), _const_spec(subln2.shape)],
        out_specs=pl.BlockSpec((1, tq, npair * LANE), lambda bi, p, qi: (bi, qi, p)),
        out_shape=jax.ShapeDtypeStruct((b, s, pairs * LANE), _BF16),
        scratch_shapes=([pltpu.VMEM((tq, tq), _F32)] * (4 * npair)
                        + [pltpu.VMEM((LANE, tq), _F32)] * (4 * npair)),
        compiler_params=_cparams(3),
        name="diff_attention",
    )(qt, kvar, vt4, lam, subln2)


def _ssm_pack_kernel(x_ref, o_ref, *, ct, lane_tiles):
    L, P = SSM_CHUNK, SSM_GROUP
    per_lane_tile = LANE // P
    for s_in in range(L):
        for lt in range(lane_tiles):
            xs = x_ref[pl.ds(s_in * lane_tiles + lt, ct, stride=L * lane_tiles), :]
            xt = xs.T
            for gi in range(per_lane_tile):
                o_ref[lt * per_lane_tile + gi, s_in * P:(s_in + 1) * P, :] = (
                    xt[gi * P:(gi + 1) * P, :].astype(o_ref.dtype))


def _ssm_unpack_kernel(y_ref, o_ref, *, ct, lane_tiles):
    L, P = SSM_CHUNK, SSM_GROUP
    per_lane_tile = LANE // P
    for t_in in range(L):
        for lt in range(lane_tiles):
            z = jnp.concatenate([y_ref[lt * per_lane_tile + gi, t_in * P:(t_in + 1) * P, :].astype(_F32)
                                 for gi in range(per_lane_tile)], axis=0)
            o_ref[pl.ds(t_in * lane_tiles + lt, ct, stride=L * lane_tiles), :] = z.T


def _ssm_tiles(t, batch):
    chunks = t // batch // SSM_CHUNK
    ct = min(LANE, chunks)
    assert chunks % ct == 0
    return chunks, ct


def _ssm_pack(us, batch, groups):
    lane_tiles = groups * SSM_GROUP // LANE
    t = us.shape[0] // lane_tiles
    chunks, ct = _ssm_tiles(t, batch)
    return pl.pallas_call(
        functools.partial(_ssm_pack_kernel, ct=ct, lane_tiles=lane_tiles),
        grid=(t // (ct * SSM_CHUNK),),
        in_specs=[pl.BlockSpec((ct * SSM_CHUNK * lane_tiles, LANE), lambda i: (i, 0))],
        out_specs=pl.BlockSpec((groups, SSM_CHUNK * SSM_GROUP, ct), lambda i: (0, 0, i)),
        out_shape=jax.ShapeDtypeStruct((groups, SSM_CHUNK * SSM_GROUP, batch * chunks), _BF16),
        compiler_params=_cparams(1),
        name="ssm_pack",
    )(us)


def _ssm_unpack(yt, batch):
    groups, w, r = yt.shape
    t = r * SSM_CHUNK
    chunks, ct = _ssm_tiles(t, batch)
    lane_tiles = groups * SSM_GROUP // LANE
    return pl.pallas_call(
        functools.partial(_ssm_unpack_kernel, ct=ct, lane_tiles=lane_tiles),
        grid=(r // ct,),
        in_specs=[pl.BlockSpec((groups, w, ct), lambda i: (0, 0, i))],
        out_specs=pl.BlockSpec((ct * SSM_CHUNK * lane_tiles, LANE), lambda i: (i, 0)),
        out_shape=jax.ShapeDtypeStruct((t * lane_tiles, LANE), _F32),
        compiler_params=_cparams(1),
        name="ssm_unpack",
    )(yt)


def _ssm_kernel(u_ref, t_ref, b_ref, c_ref, ar_ref, ai_ref, d_ref, y_ref, x_scr, xs_scr, h_scr, *, batch, chunks):
    ut = u_ref[0]
    xt = _dot(b_ref[0], ut)
    x_scr[...] = xt.T
    xs_scr[...] = jnp.concatenate([xt[SSM_STATE:], xt[:SSM_STATE]], axis=0).T
    ar, ai = ar_ref[0], ai_ref[0]
    ai_s = pltpu.roll(ai, SSM_STATE, 1)

    def step(c, carry):
        h, hs = carry
        rows = pl.ds(c, batch, stride=chunks)
        h_scr[rows, :] = h
        return ar * h + ai * hs + x_scr[rows, :], ar * hs + ai_s * h + xs_scr[rows, :]

    zero = jnp.zeros((batch, 2 * SSM_STATE), _F32)
    lax.fori_loop(0, chunks, step, (zero, zero), unroll=SCAN_UNROLL)
    ht = h_scr[...].T.astype(_BF16)
    yt = _dot(t_ref[0], ut) + _dot(c_ref[0], ht) + ut.astype(_F32) * d_ref[0]
    y_ref[0] = jax.nn.gelu(yt).astype(y_ref.dtype)


def _ssm(u_t, tmat_t, bmat_t, cmat_t, ar, ai, d_col, batch):
    g, w, r = u_t.shape
    per_g = lambda shape: pl.BlockSpec((1,) + shape, lambda i: (i, 0, 0))
    return pl.pallas_call(
        functools.partial(_ssm_kernel, batch=batch, chunks=r // batch),
        grid=(g,),
        in_specs=[per_g((w, r)), per_g((w, w)), per_g((2 * SSM_STATE, w)), per_g((w, 2 * SSM_STATE)),
                  per_g((1, 2 * SSM_STATE)), per_g((1, 2 * SSM_STATE)), per_g((w, 1))],
        out_specs=per_g((w, r)),
        out_shape=jax.ShapeDtypeStruct((g, w, r), _BF16),
        scratch_shapes=[pltpu.VMEM((r, 2 * SSM_STATE), _F32)] * 3,
        compiler_params=_cparams(1),
        name="ssm_scan",
    )(u_t, tmat_t, bmat_t, cmat_t, ar, ai, d_col)


def _ssm_matrices(a_re, a_im, log_dt, b_re, b_im, c_re, c_im, d):
    L, P, N = SSM_CHUNK, SSM_GROUP, SSM_STATE
    G = a_re.shape[0]
    A = lax.complex(a_re.astype(_F32), a_im.astype(_F32))
    dt = jnp.exp(log_dt.astype(_F32))[:, None]
    a_bar = jnp.exp(A * dt)
    b_bar = ((a_bar - 1.0) / A)[..., None] * lax.complex(b_re.astype(_F32), b_im.astype(_F32))
    C = lax.complex(c_re.astype(_F32), c_im.astype(_F32))
    steps = jnp.arange(L + 1, dtype=_F32)
    apow = jnp.exp((A * dt)[None] * steps[:, None, None])
    m = jnp.real(jnp.einsum('gpn,dgn,gnq->dgpq', C, apow[:L], b_bar))
    s_idx = jnp.arange(L)[:, None]
    t_idx = jnp.arange(L)[None, :]
    lag = jnp.clip(t_idx - s_idx, 0, L - 1)
    tfull = jnp.where((t_idx >= s_idx)[:, :, None, None, None], m[lag], 0.0)
    tmat = tfull.transpose(2, 0, 4, 1, 3).reshape(G, L * P, L * P)
    bm = apow[L - 1 - jnp.arange(L)][:, :, :, None] * b_bar[None]
    bm = bm.transpose(1, 0, 3, 2).reshape(G, L * P, N)
    bmat = jnp.concatenate([jnp.real(bm), jnp.imag(bm)], axis=-1)
    cm = C[None] * apow[1:L + 1][:, :, None, :]
    cm = cm.transpose(1, 3, 0, 2).reshape(G, N, L * P)
    cmat = jnp.concatenate([jnp.real(cm), -jnp.imag(cm)], axis=1)
    al = apow[L]
    ar = jnp.concatenate([jnp.real(al), jnp.real(al)], axis=-1)[:, None, :]
    ai = jnp.concatenate([-jnp.imag(al), jnp.imag(al)], axis=-1)[:, None, :]
    dflat = jnp.tile(d.astype(_F32).reshape(G, 1, P), (1, L, 1)).reshape(G, 1, L * P)
    tr = lambda a: a.transpose(0, 2, 1).astype(_BF16)
    return tr(tmat), tr(bmat), tr(cmat), ar, ai, dflat.reshape(G, L * P, 1)


def _layernorm(z, g, b):
    mu = jnp.mean(z, axis=-1, keepdims=True)
    zc = z - mu
    var = jnp.mean(zc * zc, axis=-1, keepdims=True)
    return zc * lax.rsqrt(var + LN_EPS) * g + b


def _merge_kernel(om_ref, od_ref, ys_ref, g_ref, x_ref, wmo_ref, wdo_ref, wglu_ref, wo_ref, lng_ref, lnb_ref,
                  rw_ref, rb_ref, x1_ref, idx_ref, wts_ref, cnt_ref, *, alpha, d_model):
    def branch_dots(rows):
        n_rows = rows.stop - rows.start
        lt = wglu_ref.shape[0] // LANE
        ys = jnp.concatenate([ys_ref[pl.ds(rows.start * lt + c, n_rows, stride=lt), :] for c in range(lt)], axis=1)
        return (_dot(om_ref[rows, :], wmo_ref[...]), _dot(od_ref[rows, :], wdo_ref[...]),
                _dot(ys.astype(_BF16), wglu_ref[...]))

    def gate_merge(rows, ya, yb, gl):
        yc = gl[:, :d_model] * jax.nn.sigmoid(gl[:, d_model:])
        merged = (g_ref[rows, 0:d_model].astype(_F32) * ya + g_ref[rows, d_model:2 * d_model].astype(_F32) * yb
                  + g_ref[rows, 2 * d_model:3 * d_model].astype(_F32) * yc)
        return merged.astype(_BF16)

    def out_norm(rows, merged):
        z = alpha * x_ref[rows, :] + _dot(merged, wo_ref[...])
        x1 = _layernorm(z, lng_ref[...], lnb_ref[...])
        for c in range(d_model // LANE):
            x1_ref[pl.ds(rows.start * TOKEN_TILE + c, rows.stop - rows.start, stride=TOKEN_TILE), :] = (
                x1[:, c * LANE:(c + 1) * LANE])
        return x1

    def router_logits(x1):
        return _dot_nt(rw_ref[...], x1, precision=lax.Precision.HIGHEST)

    def route(rows, logits):
        scores = jax.nn.sigmoid(logits)
        biased = scores + rb_ref[...]
        epg = logits.shape[0] // N_GROUPS
        a = [biased[j * N_GROUPS:(j + 1) * N_GROUPS] for j in range(epg)]
        sc = [scores[j * N_GROUPS:(j + 1) * N_GROUPS] for j in range(epg)]
        gs = None
        for i in range(epg):
            for j in range(i + 1, epg):
                pair = a[i] + a[j]
                gs = pair if gs is None else jnp.maximum(gs, pair)
        giota = lax.broadcasted_iota(jnp.int32, gs.shape, 0)
        gmax = jnp.max(gs, axis=0, keepdims=True)
        gsel = jnp.min(jnp.where(gs == gmax, giota, N_GROUPS), axis=0, keepdims=True)
        hot = giota == gsel
        val = [jnp.sum(jnp.where(hot, a[j], 0.0), axis=0, keepdims=True) for j in range(epg)]
        raw = [jnp.sum(jnp.where(hot, sc[j], 0.0), axis=0, keepdims=True) for j in range(epg)]
        b1, i1, w1 = val[0], jnp.zeros_like(gsel), raw[0]
        for j in range(1, epg):
            take = val[j] > b1
            b1 = jnp.where(take, val[j], b1)
            i1 = jnp.where(take, j, i1)
            w1 = jnp.where(take, raw[j], w1)
        b2 = jnp.full_like(b1, -jnp.inf)
        i2 = jnp.zeros_like(gsel)
        w2 = jnp.zeros_like(w1)
        for j in range(epg):
            take = (i1 != j) & (val[j] > b2)
            b2 = jnp.where(take, val[j], b2)
            i2 = jnp.where(take, j, i2)
            w2 = jnp.where(take, raw[j], w2)
        wsum = w1 + w2
        e1, e2 = gsel * epg + i1, gsel * epg + i2
        idx_ref[:, rows] = jnp.concatenate([e1, e2], axis=0)
        wts_ref[:, rows] = jnp.concatenate([w1 / wsum, w2 / wsum], axis=0)
        eiota = lax.broadcasted_iota(jnp.int32, logits.shape, 0)
        hit = (eiota == e1).astype(jnp.int32) + (eiota == e2).astype(jnp.int32)
        part = hit[:, 0:LANE]
        for c in range(1, hit.shape[1] // LANE):
            part = part + hit[:, c * LANE:(c + 1) * LANE]
        return part

    tm = x_ref.shape[0]
    sub = tm // MERGE_SUBTILES
    rows = [slice(k * sub, (k + 1) * sub) for k in range(MERGE_SUBTILES)]
    dots = [branch_dots(r) for r in rows]
    merged = [gate_merge(r, *d) for r, d in zip(rows, dots)]
    x1s = [out_norm(r, m) for r, m in zip(rows, merged)]
    logits = [router_logits(x1) for x1 in x1s]
    parts = [route(r, lg) for r, lg in zip(rows, logits)]

    @pl.when(pl.program_id(0) == 0)
    def _():
        cnt_ref[...] = jnp.zeros_like(cnt_ref)

    cnt_ref[...] += functools.reduce(lambda p, q: p + q, parts)


def _merge(om, od, ys, gates, x2d, wmo, wdo, wglu, wo, lng, lnb, rw, rb, tm, alpha):
    t, d_model = x2d.shape
    n_experts = rw.shape[0]
    assert d_model == TOKEN_TILE * LANE and tm % MERGE_SUBTILES == 0
    row = lambda w: pl.BlockSpec((tm, w), lambda i: (i, 0))
    col = pl.BlockSpec((TOP_K, tm), lambda i: (0, i))
    return pl.pallas_call(
        functools.partial(_merge_kernel, alpha=alpha, d_model=d_model),
        grid=(t // tm,),
        in_specs=[row(om.shape[1]), row(od.shape[1]),
                  pl.BlockSpec((tm * (ys.shape[0] // t), LANE), lambda i: (i, 0)), row(gates.shape[1]), row(d_model),
                  _const_spec(wmo.shape), _const_spec(wdo.shape), _const_spec(wglu.shape), _const_spec(wo.shape),
                  _const_spec(lng.shape), _const_spec(lnb.shape), _const_spec(rw.shape), _const_spec(rb.shape)],
        out_specs=(pl.BlockSpec((tm * TOKEN_TILE, LANE), lambda i: (i, 0)), col, col,
                   pl.BlockSpec((n_experts, LANE), lambda i: (0, 0))),
        out_shape=(jax.ShapeDtypeStruct((t * TOKEN_TILE, LANE), _F32), jax.ShapeDtypeStruct((TOP_K, t), jnp.int32),
                   jax.ShapeDtypeStruct((TOP_K, t), _F32), jax.ShapeDtypeStruct((n_experts, LANE), jnp.int32)),
        compiler_params=_cparams(1),
        name="merge_router",
    )(om, od, ys, gates, x2d, wmo, wdo, wglu, wo, lng, lnb, rw, rb)


DMA_UNROLL = 16


def _moe_kernel(be_ref, cnt_ref, nu_ref, tok_ref, tokn_ref, dst_ref, x_hbm, wg_ref, wu_ref, wd_ref, out_hbm,
                xbuf, ybuf, wgb, wub, wdb, sem_in, sem_out, *, mb):
    i = pl.program_id(0)
    n_used = nu_ref[0]
    slot = lax.rem(i, 2)

    def for_rows(start_row):
        def body(j, carry):
            for u in range(DMA_UNROLL):
                start_row(j * DMA_UNROLL + u, u % 2)
            return carry
        lax.fori_loop(0, mb // DMA_UNROLL, body, 0)

    def for_first_rows(n, fn):
        def body(r, carry):
            fn(r)
            return carry
        lax.fori_loop(0, n, body, 0)

    def tile_rows(tok):
        return pl.ds(pl.multiple_of(tok * TOKEN_TILE, TOKEN_TILE), TOKEN_TILE)

    def gather_row(t_ref, s, r):
        return pltpu.make_async_copy(x_hbm.at[tile_rows(t_ref[0, 0, r]), :], xbuf.at[s, tile_rows(r), :],
                                     sem_in.at[s])

    def scatter_row(s, r, dst):
        return pltpu.make_async_copy(ybuf.at[s, tile_rows(r), :], out_hbm.at[tile_rows(dst), :], sem_out.at[s])

    def issue_gather(t_ref, s):
        for_rows(lambda r, pr: gather_row(t_ref, s, r).start(priority=pr))

    def wait_gather(s):
        pltpu.make_async_copy(x_hbm.at[pl.ds(0, mb * TOKEN_TILE), :], xbuf.at[s], sem_in.at[s]).wait()

    def issue_scatter(s, cnt):
        @pl.when(cnt == mb)
        def _():
            for_rows(lambda r, pr: scatter_row(s, r, dst_ref[0, 0, r]).start(priority=pr))

        @pl.when(cnt < mb)
        def _():
            for_first_rows(cnt, lambda r: scatter_row(s, r, dst_ref[0, 0, r]).start())

    def wait_scatter(s, cnt):
        @pl.when(cnt == mb)
        def _():
            pltpu.make_async_copy(ybuf.at[s], out_hbm.at[pl.ds(0, mb * TOKEN_TILE), :], sem_out.at[s]).wait()

        @pl.when(cnt < mb)
        def _():
            for_first_rows(cnt, lambda r: scatter_row(s, r, 0).wait())

    @pl.when(i < n_used)
    def _():
        @pl.when(i == 0)
        def _():
            issue_gather(tok_ref, 0)

        @pl.when(i + 1 < n_used)
        def _():
            issue_gather(tokn_ref, 1 - slot)

        @pl.when((i == 0) | (be_ref[i] != be_ref[jnp.maximum(i - 1, 0)]))
        def _():
            wgb[...] = wg_ref[0, 0].astype(_BF16)
            wub[...] = wu_ref[0, 0].astype(_BF16)
            wdb[...] = wd_ref[0, 0].astype(_BF16)

        wait_gather(slot)

        @pl.when(i >= 2)
        def _():
            wait_scatter(slot, cnt_ref[jnp.maximum(i - 2, 0)])

        xb = jnp.concatenate([xbuf[slot, pl.ds(c, mb, stride=TOKEN_TILE), :] for c in range(TOKEN_TILE)],
                             axis=1).astype(_BF16)
        h = jax.nn.silu(_dot(xb, wgb[...])) * _dot(xb, wub[...])
        y = _dot(h.astype(_BF16), wdb[...])
        for c in range(TOKEN_TILE):
            ybuf[slot, pl.ds(c, mb, stride=TOKEN_TILE), :] = y[:, c * LANE:(c + 1) * LANE]
        issue_scatter(slot, cnt_ref[i])

        @pl.when(i == n_used - 1)
        def _():
            @pl.when(i >= 1)
            def _():
                wait_scatter(1 - slot, cnt_ref[jnp.maximum(i - 1, 0)])
            wait_scatter(slot, cnt_ref[i])


def _moe(block_e, block_cnt, n_used, row_tok, row_dst, x1t, wg, wu, wd, layer):
    t = x1t.shape[0] // TOKEN_TILE
    d_model = TOKEN_TILE * LANE
    n_blocks, mb = row_tok.shape
    ff = wg.shape[-1]
    assert mb % DMA_UNROLL == 0
    smem_row = lambda f: pl.BlockSpec((1, 1, mb), lambda i, be, cnt, nu: (f(i), 0, 0), memory_space=pltpu.SMEM)
    wspec = lambda a, b: pl.BlockSpec((1, 1, a, b), lambda i, be, cnt, nu: (layer, be[i], 0, 0))
    grid_spec = pltpu.PrefetchScalarGridSpec(
        num_scalar_prefetch=3,
        grid=(n_blocks,),
        in_specs=[smem_row(lambda i: i), smem_row(lambda i: jnp.minimum(i + 1, n_blocks - 1)), smem_row(lambda i: i),
                  pl.BlockSpec(memory_space=pl.ANY),
                  wspec(d_model, ff), wspec(d_model, ff), wspec(ff, d_model)],
        out_specs=pl.BlockSpec(memory_space=pl.ANY),
        scratch_shapes=[pltpu.VMEM((2, mb * TOKEN_TILE, LANE), _F32), pltpu.VMEM((2, mb * TOKEN_TILE, LANE), _F32),
                        pltpu.VMEM((d_model, ff), _BF16), pltpu.VMEM((d_model, ff), _BF16),
                        pltpu.VMEM((ff, d_model), _BF16),
                        pltpu.SemaphoreType.DMA((2,)), pltpu.SemaphoreType.DMA((2,))])
    tok3 = row_tok.reshape(n_blocks, 1, mb)
    return pl.pallas_call(
        functools.partial(_moe_kernel, mb=mb),
        grid_spec=grid_spec,
        out_shape=jax.ShapeDtypeStruct((TOP_K * t * TOKEN_TILE, LANE), _F32),
        compiler_params=_cparams(1),
        name="moe_experts",
    )(block_e, block_cnt, n_used, tok3, tok3, row_dst.reshape(n_blocks, 1, mb), x1t, wg, wu, wd)


def _moe_plan(idx, counts, mb):
    k, t = idx.shape
    a_total = k * t
    n_experts = counts.shape[0]
    e_flat = idx.reshape(a_total)
    order = jnp.argsort(e_flat).astype(jnp.int32)
    starts = jnp.cumsum(counts) - counts
    padded = (counts + mb - 1) // mb * mb
    pad_ends = jnp.cumsum(padded)
    pad_starts = pad_ends - padded
    n_blocks = -(-a_total // mb) + n_experts
    blk = jnp.arange(n_blocks, dtype=jnp.int32)
    blk_start = blk * mb
    block_e = jnp.minimum(jnp.sum((pad_ends[None, :] <= blk_start[:, None]).astype(jnp.int32), axis=1),
                          n_experts - 1)
    off0 = blk_start - pad_starts[block_e]
    cnt = jnp.clip(counts[block_e] - off0, 0, mb)
    n_used = jnp.sum((cnt > 0).astype(jnp.int32)).reshape(1)
    lane = jnp.arange(mb, dtype=jnp.int32)[None, :]
    valid = lane < cnt[:, None]
    src = jnp.clip((starts[block_e] + off0)[:, None] + lane, 0, a_total - 1)
    a_r = order[src]
    row_dst = jnp.where(valid, a_r, 0).astype(jnp.int32)
    row_tok = jnp.where(valid, a_r % t, 0).astype(jnp.int32)
    return block_e.astype(jnp.int32), cnt.astype(jnp.int32), n_used, row_tok, row_dst


def _final_kernel(x_ref, y0_ref, y1_ref, w_ref, lng_ref, lnb_ref, o_ref, *, alpha):
    tm = o_ref.shape[0]

    def tokens(ref):
        return jnp.concatenate([ref[pl.ds(c, tm, stride=TOKEN_TILE), :] for c in range(TOKEN_TILE)], axis=1)

    w = w_ref[...]
    z = alpha * tokens(x_ref) + (tokens(y0_ref) * w[:, 0:1] + tokens(y1_ref) * w[:, 1:2])
    o_ref[...] = _layernorm(z, lng_ref[...], lnb_ref[...])


def _final(x1t, yslots, wts_t, lng, lnb, tm, alpha):
    t = x1t.shape[0] // TOKEN_TILE
    d_model = TOKEN_TILE * LANE
    nt = t // tm
    tile = lambda f: pl.BlockSpec((tm * TOKEN_TILE, LANE), lambda i: (f(i), 0))
    return pl.pallas_call(
        functools.partial(_final_kernel, alpha=alpha),
        grid=(nt,),
        in_specs=[tile(lambda i: i), tile(lambda i: i), tile(lambda i: nt + i),
                  pl.BlockSpec((tm, TOP_K), lambda i: (i, 0)),
                  _const_spec(lng.shape), _const_spec(lnb.shape)],
        out_specs=pl.BlockSpec((tm, d_model), lambda i: (i, 0)),
        out_shape=jax.ShapeDtypeStruct((t, d_model), _F32),
        compiler_params=_cparams(1),
        name="moe_residual_ln",
    )(x1t, yslots, yslots, wts_t, lng, lnb)


def _prep_w_in(w):
    d_model = w.shape[0]
    seg = DIFF_HEADS * 2 * DIFF_HEAD_DIM
    o = 0
    cq = w[:, o:o + MLA_Q_LORA]; o += MLA_Q_LORA
    ckv = w[:, o:o + MLA_KV_LORA]; o += MLA_KV_LORA
    kr = w[:, o:o + MLA_ROPE]; o += MLA_ROPE
    rest = w[:, o:]
    z = lambda n: jnp.zeros((d_model, n), w.dtype)
    kr_chunk = jnp.concatenate([z(MLA_NOPE), kr, z(LANE - MLA_NOPE - MLA_ROPE)], axis=1)
    assert MLA_Q_LORA + MLA_KV_LORA + LANE == 4 * LANE and rest.shape[1] == 4 * seg + N_BRANCH * d_model
    return jnp.concatenate([cq, ckv, kr_chunk, rest], axis=1).astype(_BF16)


def _prep_w_uq(w):
    dq = MLA_NOPE + MLA_ROPE
    w3 = w.reshape(w.shape[0], MLA_HEADS, dq)
    w3 = jnp.pad(w3, ((0, 0), (0, 0), (0, LANE - dq)))
    return w3.reshape(w.shape[0], MLA_HEADS * LANE).astype(_BF16)


def _prep_w_ukv(w):
    w3 = w.reshape(w.shape[0], MLA_HEADS, MLA_NOPE + MLA_V)
    wk = jnp.pad(w3[:, :, :MLA_NOPE], ((0, 0), (0, 0), (0, LANE - MLA_NOPE)))
    wv = w3[:, :, MLA_NOPE:]
    return (wk.reshape(w.shape[0], MLA_HEADS * LANE).astype(_BF16),
            wv.reshape(w.shape[0], MLA_HEADS * MLA_V).astype(_BF16))


def kernel(x, positions, w_in, b_gate, mla_q_norm, mla_w_uq, mla_kv_norm, mla_w_ukv, mla_w_out,
           diff_lambda, diff_subln, diff_w_out, ssm_a_re, ssm_a_im, ssm_log_dt, ssm_b_re, ssm_b_im,
           ssm_c_re, ssm_c_im, ssm_d, ssm_w_glu, w_o, ln_gain, ln_bias, router_w, router_bias,
           moe_w_gate, moe_w_up, moe_w_down):
    B, S, D = x.shape
    T = B * S
    depth = w_in.shape[0]
    n_experts = router_w.shape[1]
    epg = n_experts // N_GROUPS
    alpha = float((2 * depth) ** 0.25)
    tm = min(ROW_TILE, T)
    tq = min(ATTN_TILE, S)
    L, P = SSM_CHUNK, SSM_GROUP
    G = ssm_a_re.shape[1]
    assert T % tm == 0 and S % tq == 0 and S % L == 0 and (TOP_K * T) % MOE_ROWS == 0
    assert MLA_HEADS * MLA_V == DIFF_HEADS * 2 * DIFF_HEAD_DIM == G * P

    (inv_a, sel_a), (inv_d, sel_d) = _rope_lane_patterns()
    pos_col = positions.reshape(T, 1).astype(jnp.int32)
    taba = _rope_tables(pos_col, inv_a, sel_a, tm)
    tabd = _rope_tables(pos_col, inv_d, sel_d, tm)

    rw = router_w.T.reshape(N_GROUPS, epg, D).transpose(1, 0, 2).reshape(n_experts, D).astype(_F32)
    rb = router_bias.reshape(N_GROUPS, epg).T.reshape(n_experts, 1).astype(_F32)

    x2d = x.reshape(T, D)
    for l in range(depth):
        lambda_init = 0.8 - 0.6 * math.exp(-0.3 * l)
        w1 = _prep_w_in(w_in[l])
        wuq = _prep_w_uq(mla_w_uq[l])
        wk, wv = _prep_w_ukv(mla_w_ukv[l])
        qmt, km, vmt, qdt, kdv, vdt, us, gates = _inproj(
            x2d, taba, tabd, w1, mla_q_norm[l][None].astype(_F32), wuq, mla_kv_norm[l][None].astype(_F32),
            wk, wv, b_gate[l].reshape(1, N_BRANCH * D).astype(_F32), tq)

        o_mla = _mla_attention(qmt, km.reshape(B, S, -1), vmt, B, tq)
        subln2 = jnp.tile(diff_subln[l].astype(_F32), 2)[None]
        o_diff = _diff_attention(qdt, kdv.reshape(4, B, S, -1), vdt, diff_lambda[l].astype(_F32), subln2,
                                 B, tq, lambda_init)

        mats = _ssm_matrices(ssm_a_re[l], ssm_a_im[l], ssm_log_dt[l], ssm_b_re[l], ssm_b_im[l],
                             ssm_c_re[l], ssm_c_im[l], ssm_d[l])
        ys = _ssm_unpack(_ssm(_ssm_pack(us, B, G), *mats, batch=B), B)

        x1, idx, wts, cnt_part = _merge(
            o_mla.reshape(T, -1), o_diff.reshape(T, -1), ys, gates, x2d,
            mla_w_out[l].astype(_BF16), diff_w_out[l].astype(_BF16), ssm_w_glu[l].astype(_BF16),
            w_o[l].astype(_BF16), ln_gain[l, 0][None].astype(_F32), ln_bias[l, 0][None].astype(_F32),
            rw, rb, min(MERGE_TILE, T), alpha)

        plan = _moe_plan(idx, jnp.sum(cnt_part, axis=1), MOE_ROWS)
        yslots = _moe(*plan, x1, moe_w_gate.astype(_F32), moe_w_up.astype(_F32), moe_w_down.astype(_F32), l)
        x2d = _final(x1, yslots, wts.T, ln_gain[l, 1][None].astype(_F32), ln_bias[l, 1][None].astype(_F32),
                     tm, alpha)
    return x2d.reshape(B, S, D)
```
